```python
import math
import numpy as np
import jax
import jax.numpy as jnp
from jax import lax

D_MODEL = 4096
BATCH = 4
SEQ = 4096
DEPTH = 1

PLE_DIM = 256
EPS = 1e-6
SSM_WIDTH = D_MODEL // 2
SSM_GROUP = 16
SSM_GROUPS = SSM_WIDTH // SSM_GROUP
SSM_STATE = 64
SSM_CHUNK = 128
ATTN_HEADS = 16
HEAD_DIM = 128
ATTN_WIDTH = ATTN_HEADS * HEAD_DIM
IDX_HEADS = 32
IDX_DIM = 128
IDX_ROPE_DIM = 64
TOPK_MAX = 256
Q_BLOCK = 64
ROPE_THETA = 10000.0
N_BRANCHES = 2
IN_SIZES = (SSM_WIDTH, ATTN_WIDTH, ATTN_WIDTH, ATTN_WIDTH, IDX_HEADS * IDX_DIM, IDX_DIM, IDX_HEADS, N_BRANCHES * D_MODEL)
IN_COLS = SSM_WIDTH + 3 * ATTN_WIDTH + IDX_HEADS * IDX_DIM + IDX_DIM + IDX_HEADS + N_BRANCHES * D_MODEL
N_EXPERTS = 32
TOP_K = 4
D_EXPERT = 1024
SWIGLU_ALPHA = 1.702
SWIGLU_LIMIT = 7.0
EXPERT_BLOCK = 256

kernel_name = "hybrid_s5_dsa_moe_ple_block"


def _rmsnorm(x, g):
    xf = x.astype(jnp.float32)
    y = xf * lax.rsqrt(jnp.mean(xf * xf, axis=-1, keepdims=True) + EPS)
    return (y * g.astype(jnp.float32)).astype(x.dtype)


def _rope(x, pos, rot_dim):
    half = rot_dim // 2
    inv = ROPE_THETA ** (-jnp.arange(half, dtype=jnp.float32) / half)
    ang = pos.astype(jnp.float32)[:, None] * inv[None, :]
    cos = jnp.cos(ang)[:, None, :].astype(x.dtype)
    sin = jnp.sin(ang)[:, None, :].astype(x.dtype)
    xr, xp = x[..., :rot_dim], x[..., rot_dim:]
    x1, x2 = xr[..., :half], xr[..., half:]
    rot = jnp.concatenate([x1 * cos - x2 * sin, x2 * cos + x1 * sin], axis=-1)
    return jnp.concatenate([rot, xp], axis=-1)


def _cplx_combine(e1, e2):
    a1r, a1i, b1r, b1i = e1
    a2r, a2i, b2r, b2i = e2
    return (a2r * a1r - a2i * a1i,
            a2r * a1i + a2i * a1r,
            a2r * b1r - a2i * b1i + b2r,
            a2r * b1i + a2i * b1r + b2i)


def _s5_branch(u, lam_re, lam_im, log_dt, b_re, b_im, c_re, c_im, d_skip, w_glu):
    bsz, seq, _ = u.shape
    dt = jnp.exp(log_dt)[:, None]
    mag = jnp.exp(lam_re * dt)
    lb_re, lb_im = mag * jnp.cos(lam_im * dt), mag * jnp.sin(lam_im * dt)
    den = lam_re * lam_re + lam_im * lam_im
    nr, ni = lb_re - 1.0, lb_im
    f_re = (nr * lam_re + ni * lam_im) / den
    f_im = (ni * lam_re - nr * lam_im) / den
    bb_re = f_re[..., None] * b_re - f_im[..., None] * b_im
    bb_im = f_re[..., None] * b_im + f_im[..., None] * b_re
    n_chunks = seq // SSM_CHUNK
    u_chunks = u.reshape(bsz, n_chunks, SSM_CHUNK, SSM_GROUPS, SSM_GROUP).swapaxes(0, 1)
    a_re = jnp.broadcast_to(lb_re, (bsz, SSM_CHUNK, SSM_GROUPS, SSM_STATE))
    a_im = jnp.broadcast_to(lb_im, (bsz, SSM_CHUNK, SSM_GROUPS, SSM_STATE))

    def chunk_step(carry, uc):
        hr, hi = carry
        br = jnp.einsum('bcgp,gnp->bcgn', uc, bb_re)
        bi = jnp.einsum('bcgp,gnp->bcgn', uc, bb_im)
        pr, pim, sr, si = lax.associative_scan(_cplx_combine, (a_re, a_im, br, bi), axis=1)
        xr = pr * hr[:, None] - pim * hi[:, None] + sr
        xi = pr * hi[:, None] + pim * hr[:, None] + si
        y = jnp.einsum('bcgn,gpn->bcgp', xr, c_re) - jnp.einsum('bcgn,gpn->bcgp', xi, c_im)
        return (xr[:, -1], xi[:, -1]), y

    h0 = jnp.zeros((bsz, SSM_GROUPS, SSM_STATE), u.dtype)
    _, ys = lax.scan(chunk_step, (h0, h0), u_chunks)
    y = ys.swapaxes(0, 1).reshape(bsz, seq, SSM_WIDTH) + d_skip * u
    g = jax.nn.gelu(y)
    return g * jax.nn.sigmoid(g @ w_glu)


def _dsa_branch(q, k, v, q_idx, k_idx, w_idx):
    bsz, seq = q.shape[0], q.shape[1]
    topk = min(TOPK_MAX, seq // 4)
    n_blocks = seq // Q_BLOCK
    pos = jnp.arange(seq)
    k_idx32 = k_idx.astype(jnp.float32)

    def blocks(z):
        return z.reshape((bsz, n_blocks, Q_BLOCK) + z.shape[2:]).swapaxes(0, 1)

    def body(args):
        qb, qib, wb, tb = args
        sc = jnp.einsum('bqhd,bsd->bhqs', qib.astype(jnp.float32), k_idx32) * (IDX_DIM ** -0.5)
        sc = jnp.einsum('bhqs,bqh->bqs', jax.nn.relu(sc), wb.astype(jnp.float32))
        sc = jnp.where(pos[None, None, :] <= tb[None, :, None], sc, -jnp.inf)
        _, sel = lax.top_k(sc, topk)
        valid = sel <= tb[None, :, None]
        k_sel = jax.vmap(lambda kk, ii: kk[ii])(k, sel)
        v_sel = jax.vmap(lambda vv, ii: vv[ii])(v, sel)
        lg = jnp.einsum('bqhd,bqkhd->bhqk', qb, k_sel).astype(jnp.float32) * (HEAD_DIM ** -0.5)
        lg = jnp.where(valid[:, None], lg, -jnp.inf)
        pr = jax.nn.softmax(lg, axis=-1).astype(v.dtype)
        o = jnp.einsum('bhqk,bqkhd->bqhd', pr, v_sel)
        return o.reshape(bsz, Q_BLOCK, ATTN_WIDTH)

    out = lax.map(body, (blocks(q), blocks(q_idx), blocks(w_idx), pos.reshape(n_blocks, Q_BLOCK)))
    return out.swapaxes(0, 1).reshape(bsz, seq, ATTN_WIDTH)


def _moe(m, w_router, b_router, w1, b1, w2, b2):
    bsz, seq, dm = m.shape
    xt = m.reshape(-1, dm)
    n_tok = xt.shape[0]
    logits = (xt @ w_router + b_router).astype(jnp.float32)
    top_v, top_e = lax.top_k(logits, TOP_K)
    gate = jax.nn.softmax(top_v, axis=-1)
    n_assign = n_tok * TOP_K
    e_flat = top_e.reshape(-1)
    tok_flat = jnp.arange(n_assign, dtype=jnp.int32) // TOP_K
    g_flat = gate.reshape(-1)
    order = jnp.argsort(e_flat, stable=True)
    e_s, tok_s, g_s = e_flat[order], tok_flat[order], g_flat[order]
    counts = jnp.zeros((N_EXPERTS,), jnp.int32).at[e_flat].add(1)
    start = jnp.cumsum(counts) - counts
    padded = (counts + EXPERT_BLOCK - 1) // EXPERT_BLOCK * EXPERT_BLOCK
    pend = jnp.cumsum(padded)
    pstart = pend - padded
    dest = pstart[e_s] + (jnp.arange(n_assign, dtype=jnp.int32) - start[e_s])
    n_blocks = -(-n_assign // EXPERT_BLOCK) + N_EXPERTS
    n_rows = n_blocks * EXPERT_BLOCK
    row_tok = jnp.full((n_rows,), n_tok, jnp.int32).at[dest].set(tok_s)
    row_g = jnp.zeros((n_rows,), jnp.float32).at[dest].set(g_s)
    blk_e = jnp.minimum(jnp.searchsorted(pend, jnp.arange(n_blocks, dtype=jnp.int32) * EXPERT_BLOCK, side='right'), N_EXPERTS - 1)
    x_pad = jnp.concatenate([xt, jnp.zeros((1, dm), xt.dtype)], axis=0)

    def expert_block(args):
        rows, e = args
        hb = x_pad[rows] @ w1[e] + b1[e]
        x_glu = jnp.minimum(hb[:, ::2], SWIGLU_LIMIT)
        x_lin = jnp.clip(hb[:, 1::2], -SWIGLU_LIMIT, SWIGLU_LIMIT)
        act = x_glu * jax.nn.sigmoid(SWIGLU_ALPHA * x_glu) * (x_lin + 1.0)
        return act @ w2[e] + b2[e]

    y_rows = lax.map(expert_block, (row_tok.reshape(n_blocks, EXPERT_BLOCK), blk_e))
    y_rows = y_rows.reshape(n_rows, dm) * row_g[:, None].astype(y_rows.dtype)
    y = jax.ops.segment_sum(y_rows, row_tok, num_segments=n_tok + 1)[:n_tok]
    return y.reshape(bsz, seq, dm)


def _hybrid_layer(h, p_i, g_mix, w_in, b_gate, lam_re, lam_im, log_dt, b_re, b_im, c_re, c_im,
                  d_skip, w_glu, w_ssm_out, w_attn_out, w_o, g_moe, w_router, b_router,
                  w1, b1, w2, b2, g_ple, w_ple, w_ple_gate, b_ple_gate):
    bsz, seq, _ = h.shape
    a = _rmsnorm(h, g_mix)
    proj = a @ w_in
    offsets = np.cumsum(IN_SIZES)[:-1].tolist()
    u, q, k, v, qi, ki, wi, gt = jnp.split(proj, offsets, axis=-1)
    pos = jnp.arange(seq)
    q = _rope(q.reshape(bsz, seq, ATTN_HEADS, HEAD_DIM), pos, HEAD_DIM)
    k = _rope(k.reshape(bsz, seq, ATTN_HEADS, HEAD_DIM), pos, HEAD_DIM)
    v = v.reshape(bsz, seq, ATTN_HEADS, HEAD_DIM)
    qi = _rope(qi.reshape(bsz, seq, IDX_HEADS, IDX_DIM), pos, IDX_ROPE_DIM)
    ki = _rope(ki.reshape(bsz, seq, 1, IDX_DIM), pos, IDX_ROPE_DIM)[:, :, 0]
    wi = wi * (IDX_HEADS ** -0.5)
    y_ssm = _s5_branch(u, lam_re, lam_im, log_dt, b_re, b_im, c_re, c_im, d_skip, w_glu)
    y_att = _dsa_branch(q, k, v, qi, ki, wi)
    gates = jax.nn.sigmoid(gt + b_gate).reshape(bsz, seq, N_BRANCHES, D_MODEL)
    merged = gates[:, :, 0] * (y_ssm @ w_ssm_out) + gates[:, :, 1] * (y_att @ w_attn_out)
    h = h + merged @ w_o
    h = h + _moe(_rmsnorm(h, g_moe), w_router, b_router, w1, b1, w2, b2)
    h = h + (p_i @ w_ple) * jax.nn.sigmoid(_rmsnorm(h, g_ple) @ w_ple_gate + b_ple_gate)
    return h


def setup_inputs(seed: int = 0) -> dict:
    key = jax.random.key(seed)
    ks = jax.random.split(key, 32)
    f32 = jnp.float32

    def nrm(k, shape, scale):
        return jax.random.normal(k, shape, f32) * scale

    L, D, G, N, P = DEPTH, D_MODEL, SSM_GROUPS, SSM_STATE, SSM_GROUP
    E, F = N_EXPERTS, D_EXPERT
    n_idx = jnp.arange(N, dtype=f32)
    return {
        "x": nrm(ks[0], (BATCH, SEQ, D), 1.0),
        "p": nrm(ks[1], (DEPTH, BATCH, SEQ, PLE_DIM), 1.0),
        "g_mix": 1.0 + nrm(ks[2], (L, D), 0.02),
        "w_in": nrm(ks[3], (L, D, IN_COLS), D ** -0.5),
        "b_gate": nrm(ks[4], (L, N_BRANCHES * D), 0.02),
        "ssm_lambda_re": -0.5 + nrm(ks[5], (L, G, N), 0.01),
        "ssm_lambda_im": math.pi * n_idx + nrm(ks[6], (L, G, N), 0.01),
        "ssm_log_dt": jax.random.uniform(ks[7], (L, G), f32, math.log(1e-3), math.log(1e-1)),
        "ssm_b_re": nrm(ks[8], (L, G, N, P), (2 * P) ** -0.5),
        "ssm_b_im": nrm(ks[9], (L, G, N, P), (2 * P) ** -0.5),
        "ssm_c_re": nrm(ks[10], (L, G, P, N), N ** -0.5),
        "ssm_c_im": nrm(ks[11], (L, G, P, N), N ** -0.5),
        "ssm_d": 1.0 + nrm(ks[12], (L, SSM_WIDTH), 0.1),
        "ssm_w_glu": nrm(ks[13], (L, SSM_WIDTH, SSM_WIDTH), SSM_WIDTH ** -0.5),
        "w_ssm_out": nrm(ks[14], (L, SSM_WIDTH, D), SSM_WIDTH ** -0.5),
        "w_attn_out": nrm(ks[15], (L, ATTN_WIDTH, D), ATTN_WIDTH ** -0.5),
        "w_o": nrm(ks[16], (L, D, D), D ** -0.5),
        "g_moe": 1.0 + nrm(ks[17], (L, D), 0.02),
        "w_router": nrm(ks[18], (L, D, E), D ** -0.5),
        "b_router": nrm(ks[19], (L, E), 0.01),
        "w_expert_in": nrm(ks[20], (L, E, D, 2 * F), D ** -0.5),
        "b_expert_in": nrm(ks[21], (L, E, 2 * F), 0.02),
        "w_expert_out": nrm(ks[22], (L, E, F, D), F ** -0.5),
        "b_expert_out": nrm(ks[23], (L, E, D), 0.02),
        "g_ple": 1.0 + nrm(ks[24], (L, D), 0.02),
        "w_ple": nrm(ks[25], (L, PLE_DIM, D), PLE_DIM ** -0.5),
        "w_ple_gate": nrm(ks[26], (L, D, D), D ** -0.5),
        "b_ple_gate": nrm(ks[27], (L, D), 0.02),
        "g_final": 1.0 + nrm(ks[28], (D,), 0.02),
    }


def reference(x, p, g_mix, w_in, b_gate, ssm_lambda_re, ssm_lambda_im, ssm_log_dt, ssm_b_re, ssm_b_im,
              ssm_c_re, ssm_c_im, ssm_d, ssm_w_glu, w_ssm_out, w_attn_out, w_o, g_moe, w_router, b_router,
              w_expert_in, b_expert_in, w_expert_out, b_expert_out, g_ple, w_ple, w_ple_gate, b_ple_gate,
              g_final):
    h = x
    for i in range(DEPTH):
        h = _hybrid_layer(h, p[i], g_mix[i], w_in[i], b_gate[i], ssm_lambda_re[i], ssm_lambda_im[i],
                          ssm_log_dt[i], ssm_b_re[i], ssm_b_im[i], ssm_c_re[i], ssm_c_im[i], ssm_d[i],
                          ssm_w_glu[i], w_ssm_out[i], w_attn_out[i], w_o[i], g_moe[i], w_router[i],
                          b_router[i], w_expert_in[i], b_expert_in[i], w_expert_out[i], b_expert_out[i],
                          g_ple[i], w_ple[i], w_ple_gate[i], b_ple_gate[i])
    return _rmsnorm(h, g_final)
```

```python
import functools
import math

import jax
import jax.numpy as jnp
import numpy as np
from jax import lax
from jax.experimental import pallas as pl
from jax.experimental.pallas import tpu as pltpu

EPS = 1e-6
HEAD_DIM = 128
IDX_DIM = 128
IDX_ROPE_DIM = 64
TOPK_MAX = 256
ROPE_THETA = 10000.0
TOP_K = 4
SWIGLU_ALPHA = 1.702
SWIGLU_LIMIT = 7.0
LANES = 128
MXU_DIM = 256
VMEM_LIMIT = 56 * 1024 * 1024
NEG_BIG = -1e30

F32 = jnp.float32
BF16 = jnp.bfloat16


def _pick(pref, n):
    b = min(pref, n)
    while n % b:
        b -= 1
    return b


def _params(sem):
    return pltpu.CompilerParams(dimension_semantics=sem, vmem_limit_bytes=VMEM_LIMIT)


def _rms_kernel(x_ref, g_ref, o_ref):
    x = x_ref[...]
    ms = jnp.mean(x * x, axis=-1, keepdims=True)
    o_ref[...] = (x * lax.rsqrt(ms + EPS) * g_ref[...]).astype(o_ref.dtype)


def _rmsnorm(x2d, g, out_dtype, name):
    m, d = x2d.shape
    bm = _pick(256, m)
    return pl.pallas_call(
        _rms_kernel,
        out_shape=jax.ShapeDtypeStruct((m, d), out_dtype),
        grid=(m // bm,),
        in_specs=[pl.BlockSpec((bm, d), lambda i: (i, 0)), pl.BlockSpec((1, d), lambda i: (0, 0))],
        out_specs=pl.BlockSpec((bm, d), lambda i: (i, 0)),
        compiler_params=_params(("parallel",)),
        name=name,
    )(x2d, g.reshape(1, d))


def _mm_kernel(*refs, n_lhs, n_extra, epilogue):
    lhs = refs[: 2 * n_lhs]
    extras = refs[2 * n_lhs: 2 * n_lhs + n_extra]
    o_ref = refs[2 * n_lhs + n_extra]
    accs = [jnp.dot(lhs[2 * t][...], lhs[2 * t + 1][...], preferred_element_type=F32) for t in range(n_lhs)]
    epilogue(accs, extras, o_ref)


def _matmul(pairs, extras, epilogue, n_out, out_dtype, bm, bn, name):
    m = pairs[0][0].shape[0]
    bm = _pick(bm, m)
    bn = _pick(bn, n_out)
    in_specs, args = [], []
    for a, w in pairs:
        k = a.shape[1]
        in_specs += [pl.BlockSpec((bm, k), lambda i, j: (i, 0)), pl.BlockSpec((k, bn), lambda i, j: (0, j))]
        args += [a, w]
    for arr, bshape, imap in extras:
        in_specs.append(pl.BlockSpec(bshape, imap))
        args.append(arr)
    return pl.pallas_call(
        functools.partial(_mm_kernel, n_lhs=len(pairs), n_extra=len(extras), epilogue=epilogue),
        out_shape=jax.ShapeDtypeStruct((m, n_out), out_dtype),
        grid=(m // bm, n_out // bn),
        in_specs=in_specs,
        out_specs=pl.BlockSpec((bm, bn), lambda i, j: (i, j)),
        compiler_params=_params(("parallel", "arbitrary")),
        name=name,
    )(*args)


def _epi_plain(accs, extras, o_ref):
    o_ref[...] = accs[0].astype(o_ref.dtype)


def _epi_rope_full(accs, extras, o_ref):
    cos, sin = extras[0][...], extras[1][...]
    acc = accs[0]
    for h in range(acc.shape[1] // HEAD_DIM):
        xh = acc[:, h * HEAD_DIM:(h + 1) * HEAD_DIM]
        o_ref[:, h * HEAD_DIM:(h + 1) * HEAD_DIM] = (
            xh * cos + pltpu.roll(xh, HEAD_DIM // 2, 1) * sin).astype(o_ref.dtype)


def _rope_idx(xh, cos, sin_a, sin_b):
    half = IDX_ROPE_DIM // 2
    return xh * cos + pltpu.roll(xh, IDX_DIM - half, 1) * sin_a + pltpu.roll(xh, half, 1) * sin_b


def _epi_rope_idx(accs, extras, o_ref):
    cos, sin_a, sin_b = extras[0][...], extras[1][...], extras[2][...]
    acc = accs[0]
    for h in range(acc.shape[1] // IDX_DIM):
        xh = acc[:, h * IDX_DIM:(h + 1) * IDX_DIM]
        o_ref[:, h * IDX_DIM:(h + 1) * IDX_DIM] = _rope_idx(xh, cos, sin_a, sin_b).astype(o_ref.dtype)


def _epi_kiwi(accs, extras, o_ref):
    cos, sin_a, sin_b = extras[0][...], extras[1][...], extras[2][...]
    acc = accs[0]
    o_ref[:, :IDX_DIM] = _rope_idx(acc[:, :IDX_DIM], cos, sin_a, sin_b).astype(o_ref.dtype)
    o_ref[:, IDX_DIM:] = acc[:, IDX_DIM:].astype(o_ref.dtype)


def _epi_sigmoid_bias(accs, extras, o_ref):
    o_ref[...] = jax.nn.sigmoid(accs[0] + extras[0][...]).astype(o_ref.dtype)


def _epi_glu(accs, extras, o_ref):
    g = extras[0][...].astype(F32)
    o_ref[...] = (g * jax.nn.sigmoid(accs[0])).astype(o_ref.dtype)


def _epi_merge(accs, extras, o_ref):
    g0 = extras[0][...].astype(F32)
    g1 = extras[1][...].astype(F32)
    o_ref[...] = (g0 * accs[0] + g1 * accs[1]).astype(o_ref.dtype)


def _epi_residual(accs, extras, o_ref):
    o_ref[...] = (extras[0][...] + accs[0]).astype(o_ref.dtype)


def _epi_ple(accs, extras, o_ref):
    h, b = extras[0][...], extras[1][...]
    o_ref[...] = (h + accs[1] * jax.nn.sigmoid(accs[0] + b)).astype(o_ref.dtype)


def _s5_kernel(u_ref, w1_ref, w2_ref, w3_ref, ar_ref, ai_ref, d_ref, o_ref, *, n_chunks):
    hi = lax.Precision.HIGHEST
    u = u_ref[0]
    rows = u.shape[0]
    x = jnp.dot(u, w2_ref[0], precision=hi, preferred_element_type=F32)
    cidx = lax.broadcasted_iota(jnp.int32, (rows, 1), 0) % n_chunks
    half = x.shape[1] // 2
    for k in range(int(math.log2(n_chunks))):
        sh = 1 << k
        s = jnp.where(cidx >= sh, pltpu.roll(x, sh, 0), 0.0)
        x = x + ar_ref[0, k:k + 1, :] * s + ai_ref[0, k:k + 1, :] * pltpu.roll(s, half, 1)
    xprev = jnp.where(cidx >= 1, pltpu.roll(x, 1, 0), 0.0)
    y = (jnp.dot(u, w1_ref[0], precision=hi, preferred_element_type=F32)
         + jnp.dot(xprev, w3_ref[0], precision=hi, preferred_element_type=F32)
         + d_ref[0] * u)
    o_ref[0] = jax.nn.gelu(y).astype(o_ref.dtype)


def _s5_weights(lam_re, lam_im, log_dt, b_re, b_im, c_re, c_im, chunk, n_chunks):
    hi = lax.Precision.HIGHEST
    g, n, p = b_re.shape
    dt = jnp.exp(log_dt)[:, None]
    mag = jnp.exp(lam_re * dt)
    lb_re, lb_im = mag * jnp.cos(lam_im * dt), mag * jnp.sin(lam_im * dt)
    den = lam_re * lam_re + lam_im * lam_im
    nr, ni = lb_re - 1.0, lb_im
    f_re = (nr * lam_re + ni * lam_im) / den
    f_im = (ni * lam_re - nr * lam_im) / den
    bb_re = f_re[..., None] * b_re - f_im[..., None] * b_im
    bb_im = f_re[..., None] * b_im + f_im[..., None] * b_re

    def power(d):
        m = jnp.exp(lam_re * dt * d[:, None, None])
        ang = lam_im * dt * d[:, None, None]
        return m * jnp.cos(ang), m * jnp.sin(ang)

    pw_re, pw_im = power(jnp.arange(chunk + 1, dtype=F32))
    cp_re = c_re[None] * pw_re[:, :, None, :] - c_im[None] * pw_im[:, :, None, :]
    cp_im = c_re[None] * pw_im[:, :, None, :] + c_im[None] * pw_re[:, :, None, :]
    kd = (jnp.einsum('dgpn,gnj->dgpj', cp_re[:chunk], bb_re, precision=hi)
          - jnp.einsum('dgpn,gnj->dgpj', cp_im[:chunk], bb_im, precision=hi))
    s_idx = np.arange(chunk)[:, None]
    t_idx = np.arange(chunk)[None, :]
    lag = np.clip(t_idx - s_idx, 0, chunk - 1)
    w1 = jnp.where((t_idx >= s_idx)[:, :, None, None, None], kd[lag], 0.0)
    w1 = w1.transpose(2, 0, 4, 1, 3).reshape(g, chunk * p, chunk * p)
    rev_re, rev_im = pw_re[:chunk][::-1], pw_im[:chunk][::-1]
    z_re = rev_re[..., None] * bb_re[None] - rev_im[..., None] * bb_im[None]
    z_im = rev_re[..., None] * bb_im[None] + rev_im[..., None] * bb_re[None]
    w2 = jnp.concatenate([z_re, z_im], axis=2).transpose(1, 0, 3, 2).reshape(g, chunk * p, 2 * n)
    w3 = jnp.concatenate([cp_re[1:], -cp_im[1:]], axis=3)
    w3 = w3.transpose(1, 3, 0, 2).reshape(g, 2 * n, chunk * p)
    n_steps = int(math.log2(n_chunks))
    a_re, a_im = power(chunk * (2.0 ** jnp.arange(n_steps, dtype=F32)))
    ar = jnp.concatenate([a_re, a_re], axis=2).transpose(1, 0, 2)
    ai = jnp.concatenate([-a_im, a_im], axis=2).transpose(1, 0, 2)
    return w1, w2, w3, ar, ai


def _s5_branch(u2d, bsz, seq, lam_re, lam_im, log_dt, b_re, b_im, c_re, c_im, d_skip):
    g, n, p = b_re.shape
    chunk = MXU_DIM // p
    n_chunks = seq // chunk
    assert n_chunks & (n_chunks - 1) == 0 and 2 * n == LANES
    w1, w2, w3, ar, ai = _s5_weights(lam_re, lam_im, log_dt, b_re, b_im, c_re, c_im, chunk, n_chunks)
    n_steps = ar.shape[1]
    rows = bsz * n_chunks
    lp = chunk * p
    ug = u2d.reshape(bsz, n_chunks, chunk, g, p).transpose(3, 0, 1, 2, 4).reshape(g, rows, lp)
    dg = jnp.tile(d_skip.reshape(g, 1, p), (1, chunk, 1)).reshape(g, 1, lp)
    yg = pl.pallas_call(
        functools.partial(_s5_kernel, n_chunks=n_chunks),
        out_shape=jax.ShapeDtypeStruct((g, rows, lp), BF16),
        grid=(g,),
        in_specs=[
            pl.BlockSpec((1, rows, lp), lambda i: (i, 0, 0)),
            pl.BlockSpec((1, lp, lp), lambda i: (i, 0, 0)),
            pl.BlockSpec((1, lp, 2 * n), lambda i: (i, 0, 0)),
            pl.BlockSpec((1, 2 * n, lp), lambda i: (i, 0, 0)),
            pl.BlockSpec((1, n_steps, 2 * n), lambda i: (i, 0, 0)),
            pl.BlockSpec((1, n_steps, 2 * n), lambda i: (i, 0, 0)),
            pl.BlockSpec((1, 1, lp), lambda i: (i, 0, 0)),
        ],
        out_specs=pl.BlockSpec((1, rows, lp), lambda i: (i, 0, 0)),
        compiler_params=_params(("parallel",)),
        name="s5_scan",
    )(ug, w1, w2, w3, ar, ai, dg)
    return yg.reshape(g, bsz, n_chunks, chunk, p).transpose(1, 2, 3, 0, 4).reshape(bsz * seq, g * p)


def _dsa_index_kernel(qi_ref, ki_ref, wi_ref, o_ref, key_ref, *, n_heads, topk, scale):
    qb, seq = o_ref.shape
    t0 = pl.program_id(1) * qb
    ki = ki_ref[...]
    w = wi_ref[...] * scale
    s = jnp.zeros((qb, seq), F32)
    for h in range(n_heads):
        z = lax.dot_general(qi_ref[:, h * IDX_DIM:(h + 1) * IDX_DIM], ki, (((1,), (1,)), ((), ())),
                            preferred_element_type=F32)
        s = s + w[:, h:h + 1] * jnp.maximum(z, 0.0)
    row = t0 + lax.broadcasted_iota(jnp.int32, (qb, 1), 0)
    col = lax.broadcasted_iota(jnp.int32, (1, seq), 1)
    causal = col <= row
    s = jnp.where(causal, s, -jnp.inf)
    bits = lax.bitcast_convert_type(s, jnp.int32)
    key_ref[...] = jnp.where(bits < 0, bits ^ jnp.int32(0x7FFFFFFF), bits)

    def bisect(it, thr):
        cand = thr + lax.shift_left(jnp.int32(1), jnp.int32(31) - jnp.asarray(it, jnp.int32))
        cnt = jnp.sum(jnp.where(key_ref[...] >= cand, 1.0, 0.0), axis=1, keepdims=True)
        return jnp.where(cnt >= topk, cand, thr)

    thr = lax.fori_loop(0, 32, bisect, jnp.full((qb, 1), jnp.iinfo(jnp.int32).min, jnp.int32))
    bias = jnp.where(key_ref[...] >= thr, jnp.where(causal, 0.0, NEG_BIG), NEG_BIG)
    o_ref[...] = bias.astype(o_ref.dtype)


def _dsa_attn_kernel(q_ref, k_ref, v_ref, m_ref, o_ref, *, scale):
    s = lax.dot_general(q_ref[...], k_ref[...], (((1,), (1,)), ((), ())), preferred_element_type=F32)
    s = s * scale + m_ref[...].astype(F32)
    mx = jnp.max(s, axis=1, keepdims=True)
    p = jnp.exp(s - mx)
    l = jnp.sum(p, axis=1, keepdims=True)
    o = jnp.dot(p.astype(v_ref.dtype), v_ref[...], preferred_element_type=F32)
    o_ref[...] = (o / l).astype(o_ref.dtype)


def _dsa_branch(qkv, qi, kiwi, bsz, seq, n_heads, n_idx_heads):
    n_tok = bsz * seq
    aw = n_heads * HEAD_DIM
    topk = min(TOPK_MAX, seq // 4)
    qb = _pick(128, seq)
    nq = seq // qb
    mask = pl.pallas_call(
        functools.partial(_dsa_index_kernel, n_heads=n_idx_heads, topk=float(topk),
                          scale=(IDX_DIM ** -0.5) * (n_idx_heads ** -0.5)),
        out_shape=jax.ShapeDtypeStruct((n_tok, seq), BF16),
        grid=(bsz, nq),
        in_specs=[
            pl.BlockSpec((qb, n_idx_heads * IDX_DIM), lambda b, i: (b * nq + i, 0)),
            pl.BlockSpec((seq, IDX_DIM), lambda b, i: (b, 0)),
            pl.BlockSpec((qb, IDX_DIM), lambda b, i: (b * nq + i, 1)),
        ],
        out_specs=pl.BlockSpec((qb, seq), lambda b, i: (b * nq + i, 0)),
        scratch_shapes=[pltpu.VMEM((qb, seq), jnp.int32)],
        compiler_params=_params(("parallel", "arbitrary")),
        name="dsa_index",
    )(qi, kiwi.astype(BF16), kiwi)
    qa = _pick(256, seq)
    na = seq // qa
    return pl.pallas_call(
        functools.partial(_dsa_attn_kernel, scale=HEAD_DIM ** -0.5),
        out_shape=jax.ShapeDtypeStruct((n_tok, aw), BF16),
        grid=(bsz, n_heads, na),
        in_specs=[
            pl.BlockSpec((qa, HEAD_DIM), lambda b, h, i: (b * na + i, h)),
            pl.BlockSpec((seq, HEAD_DIM), lambda b, h, i: (b, n_heads + h)),
            pl.BlockSpec((seq, HEAD_DIM), lambda b, h, i: (b, 2 * n_heads + h)),
            pl.BlockSpec((qa, seq), lambda b, h, i: (b * na + i, 0)),
        ],
        out_specs=pl.BlockSpec((qa, HEAD_DIM), lambda b, h, i: (b * na + i, h)),
        compiler_params=_params(("parallel", "parallel", "arbitrary")),
        name="dsa_attn",
    )(qkv, qkv, qkv, mask)


def _router_kernel(h_ref, g_ref, wr_ref, br_ref, m_ref, r_ref, *, n_experts):
    x = h_ref[...]
    ms = jnp.mean(x * x, axis=-1, keepdims=True)
    m = x * lax.rsqrt(ms + EPS) * g_ref[...]
    m_ref[...] = m
    logits = jnp.dot(m, wr_ref[...], precision=lax.Precision.HIGHEST, preferred_element_type=F32) + br_ref[...]
    lane = lax.broadcasted_iota(jnp.int32, logits.shape, 1)
    l = jnp.where(lane < n_experts, logits, -jnp.inf)
    vals, idxs = [], []
    for _ in range(TOP_K):
        mx = jnp.max(l, axis=1, keepdims=True)
        am = jnp.min(jnp.where(l == mx, lane, LANES), axis=1, keepdims=True)
        vals.append(mx)
        idxs.append(am)
        l = jnp.where(lane == am, -jnp.inf, l)
    exps = [jnp.exp(v - vals[0]) for v in vals]
    tot = exps[0] + exps[1] + exps[2] + exps[3]
    out = jnp.zeros(logits.shape, F32)
    for k in range(TOP_K):
        out = jnp.where(lane == k, idxs[k].astype(F32), out)
        out = jnp.where(lane == TOP_K + k, exps[k] / tot, out)
    r_ref[...] = out


def _moe_kernel(blk_e_ref, nact_ref, tok_ref, tgt_ref, g_ref, m_hbm, w1g_ref, w1l_ref, b1g_ref, b1l_ref,
                w2_ref, b2_ref, y_hbm, xbuf, xb, acc, sem_in, sem_out, *, blk, n_f):
    i = pl.program_id(0)
    f = pl.program_id(1)
    active = i < nact_ref[0]

    def row_in(r):
        return pltpu.make_async_copy(m_hbm.at[pl.ds(tok_ref[0, 0, r], 1)], xbuf.at[pl.ds(r, 1)], sem_in)

    def row_out(r):
        return pltpu.make_async_copy(acc.at[pl.ds(r, 1)], y_hbm.at[pl.ds(tgt_ref[0, 0, r], 1)], sem_out)

    @pl.when(jnp.logical_and(active, f == 0))
    def _gather():
        lax.fori_loop(0, blk, lambda r, c: (row_in(r).start(), c)[1], 0)
        lax.fori_loop(0, blk, lambda r, c: (row_in(r).wait(), c)[1], 0)
        xb[...] = xbuf[...].astype(BF16)

    @pl.when(active)
    def _compute():
        x = xb[...]
        hg = jnp.dot(x, w1g_ref[0], preferred_element_type=F32) + b1g_ref[0]
        hl = jnp.dot(x, w1l_ref[0], preferred_element_type=F32) + b1l_ref[0]
        x_glu = jnp.minimum(hg, SWIGLU_LIMIT)
        x_lin = jnp.clip(hl, -SWIGLU_LIMIT, SWIGLU_LIMIT)
        act = x_glu * jax.nn.sigmoid(SWIGLU_ALPHA * x_glu) * (x_lin + 1.0)
        contrib = jnp.dot(act.astype(BF16), w2_ref[0], preferred_element_type=F32)

        @pl.when(f == 0)
        def _():
            acc[...] = contrib + b2_ref[0]

        @pl.when(f > 0)
        def _():
            acc[...] += contrib

    @pl.when(jnp.logical_and(active, f == n_f - 1))
    def _scatter():
        acc[...] = acc[...] * g_ref[0]

        def scatter_start(r, c):
            @pl.when(tgt_ref[0, 0, r] >= 0)
            def _():
                row_out(r).start()
            return c

        def scatter_wait(r, c):
            @pl.when(tgt_ref[0, 0, r] >= 0)
            def _():
                row_out(r).wait()
            return c

        lax.fori_loop(0, blk, scatter_start, 0)
        lax.fori_loop(0, blk, scatter_wait, 0)


def _moe_branch(h1, g_moe, w_router, b_router, w1, b1, w2, b2):
    n_tok, d = h1.shape
    n_exp, _, f2 = w1.shape
    fdim = f2 // 2
    bm = _pick(256, n_tok)
    wr = jnp.zeros((d, LANES), F32).at[:, :n_exp].set(w_router)
    br = jnp.zeros((1, LANES), F32).at[0, :n_exp].set(b_router)
    m, route = pl.pallas_call(
        functools.partial(_router_kernel, n_experts=n_exp),
        out_shape=(jax.ShapeDtypeStruct((n_tok, d), F32), jax.ShapeDtypeStruct((n_tok, LANES), F32)),
        grid=(n_tok // bm,),
        in_specs=[pl.BlockSpec((bm, d), lambda i: (i, 0)), pl.BlockSpec((1, d), lambda i: (0, 0)),
                  pl.BlockSpec((d, LANES), lambda i: (0, 0)), pl.BlockSpec((1, LANES), lambda i: (0, 0))],
        out_specs=(pl.BlockSpec((bm, d), lambda i: (i, 0)), pl.BlockSpec((bm, LANES), lambda i: (i, 0))),
        compiler_params=_params(("parallel",)),
        name="moe_router",
    )(h1, g_moe.reshape(1, d), wr, br)
    top_e = route[:, :TOP_K].astype(jnp.int32)
    gate = route[:, TOP_K:2 * TOP_K]

    n_assign = n_tok * TOP_K
    blk = _pick(512, n_assign)
    n_blocks = n_assign // blk + n_exp
    n_rows = n_blocks * blk
    e_flat = top_e.reshape(-1)
    onehot = (e_flat[:, None] == jnp.arange(n_exp, dtype=jnp.int32)[None, :]).astype(jnp.int32)
    csum = jnp.cumsum(onehot, axis=0)
    rank = jnp.take_along_axis(csum, e_flat[:, None], axis=1)[:, 0] - 1
    counts = csum[-1]
    padded = (counts + blk - 1) // blk * blk
    pend = jnp.cumsum(padded)
    pstart = pend - padded
    dest = pstart[e_flat] + rank
    a_idx = jnp.arange(n_assign, dtype=jnp.int32)
    row_tok = jnp.zeros((n_rows,), jnp.int32).at[dest].set(a_idx // TOP_K)
    row_tgt = jnp.full((n_rows,), -1, jnp.int32).at[dest].set((a_idx % TOP_K) * n_tok + a_idx // TOP_K)
    row_g = jnp.zeros((n_rows,), F32).at[dest].set(gate.reshape(-1))
    n_act = (pend[-1] // blk).astype(jnp.int32)
    blk_ids = jnp.minimum(jnp.arange(n_blocks, dtype=jnp.int32), n_act - 1)
    blk_e = jnp.minimum(jnp.searchsorted(pend, blk_ids * blk, side='right'), n_exp - 1).astype(jnp.int32)

    fc = _pick(256, fdim)
    n_f = fdim // fc
    w1g = w1[:, :, 0::2].astype(BF16)
    w1l = w1[:, :, 1::2].astype(BF16)
    b1g = b1[:, 0::2].reshape(n_exp, 1, fdim)
    b1l = b1[:, 1::2].reshape(n_exp, 1, fdim)
    grid_spec = pltpu.PrefetchScalarGridSpec(
        num_scalar_prefetch=2,
        grid=(n_blocks, n_f),
        in_specs=[
            pl.BlockSpec((1, 1, blk), lambda i, f, be, na: (i, 0, 0), memory_space=pltpu.SMEM),
            pl.BlockSpec((1, 1, blk), lambda i, f, be, na: (i, 0, 0), memory_space=pltpu.SMEM),
            pl.BlockSpec((1, blk, 1), lambda i, f, be, na: (i, 0, 0)),
            pl.BlockSpec(memory_space=pl.ANY),
            pl.BlockSpec((1, d, fc), lambda i, f, be, na: (be[i], 0, f)),
            pl.BlockSpec((1, d, fc), lambda i, f, be, na: (be[i], 0, f)),
            pl.BlockSpec((1, 1, fc), lambda i, f, be, na: (be[i], 0, f)),
            pl.BlockSpec((1, 1, fc), lambda i, f, be, na: (be[i], 0, f)),
            pl.BlockSpec((1, fc, d), lambda i, f, be, na: (be[i], f, 0)),
            pl.BlockSpec((1, 1, d), lambda i, f, be, na: (be[i], 0, 0)),
        ],
        out_specs=pl.BlockSpec(memory_space=pl.ANY),
        scratch_shapes=[pltpu.VMEM((blk, d), F32), pltpu.VMEM((blk, d), BF16), pltpu.VMEM((blk, d), F32),
                        pltpu.SemaphoreType.DMA, pltpu.SemaphoreType.DMA],
    )
    y4 = pl.pallas_call(
        functools.partial(_moe_kernel, blk=blk, n_f=n_f),
        out_shape=jax.ShapeDtypeStruct((n_assign, d), F32),
        grid_spec=grid_spec,
        compiler_params=pltpu.CompilerParams(dimension_semantics=("arbitrary", "arbitrary"),
                                             vmem_limit_bytes=VMEM_LIMIT, has_side_effects=True),
        name="moe_experts",
    )(blk_e, n_act.reshape(1), row_tok.reshape(n_blocks, 1, blk), row_tgt.reshape(n_blocks, 1, blk),
      row_g.reshape(n_blocks, blk, 1), m, w1g, w1l, b1g, b1l, w2.astype(BF16), b2.reshape(n_exp, 1, d))
    return y4


def _combine_kernel(h_ref, y0_ref, y1_ref, y2_ref, y3_ref, g_ref, h2_ref, n2_ref):
    h2 = h_ref[...] + ((y0_ref[...] + y1_ref[...]) + (y2_ref[...] + y3_ref[...]))
    h2_ref[...] = h2
    ms = jnp.mean(h2 * h2, axis=-1, keepdims=True)
    n2_ref[...] = (h2 * lax.rsqrt(ms + EPS) * g_ref[...]).astype(n2_ref.dtype)


def _combine(h1, y4, g_ple):
    n_tok, d = h1.shape
    bm = _pick(128, n_tok)
    nb = n_tok // bm
    yspec = [pl.BlockSpec((bm, d), functools.partial(lambda i, k: (k * nb + i, 0), k=k)) for k in range(TOP_K)]
    return pl.pallas_call(
        _combine_kernel,
        out_shape=(jax.ShapeDtypeStruct((n_tok, d), F32), jax.ShapeDtypeStruct((n_tok, d), BF16)),
        grid=(nb,),
        in_specs=[pl.BlockSpec((bm, d), lambda i: (i, 0))] + yspec + [pl.BlockSpec((1, d), lambda i: (0, 0))],
        out_specs=(pl.BlockSpec((bm, d), lambda i: (i, 0)), pl.BlockSpec((bm, d), lambda i: (i, 0))),
        compiler_params=_params(("parallel",)),
        name="moe_combine",
    )(h1, y4, y4, y4, y4, g_ple.reshape(1, d))


def _rope_tables(seq, rot_dim, width):
    half = rot_dim // 2
    inv = ROPE_THETA ** (-jnp.arange(half, dtype=F32) / half)
    ang = jnp.arange(seq, dtype=F32)[:, None] * inv[None, :]
    return jnp.cos(ang), jnp.sin(ang), half


def _layer(h, p_i, g_mix, w_in, b_gate, lam_re, lam_im, log_dt, b_re, b_im, c_re, c_im, d_skip, w_glu,
           w_ssm_out, w_attn_out, w_o, g_moe, w_router, b_router, w1, b1, w2, b2, g_ple, w_ple,
           w_ple_gate, b_ple_gate, bsz, seq):
    n_tok, d = h.shape
    ssm_w = w_glu.shape[0]
    aw = w_attn_out.shape[0]
    n_heads = aw // HEAD_DIM
    n_idx = (w_in.shape[1] - ssm_w - 3 * aw - IDX_DIM - 2 * d) // (IDX_DIM + 1)
    o_u, o_qkv, o_qi, o_ki, o_wi, o_gt = np.cumsum([0, ssm_w, 3 * aw, n_idx * IDX_DIM, IDX_DIM, n_idx]).tolist()
    bm_t = _pick(1024, seq)
    nbt = seq // bm_t

    a = _rmsnorm(h, g_mix, BF16, "rms_mix")
    w_in16 = w_in.astype(BF16)

    u = _matmul([(a, w_in16[:, o_u:o_qkv])], [], _epi_plain, ssm_w, F32, 1024, 1024, "proj_u")

    cos, sin, _ = _rope_tables(seq, HEAD_DIM, HEAD_DIM)
    cos_f = jnp.concatenate([cos, cos], axis=1)
    sin_f = jnp.concatenate([-sin, sin], axis=1)
    tab = lambda t: (t, (bm_t, LANES), lambda i, j: (i % nbt, 0))
    qk = _matmul([(a, w_in16[:, o_qkv:o_qkv + 2 * aw])], [tab(cos_f), tab(sin_f)], _epi_rope_full,
                 2 * aw, BF16, bm_t, 1024, "proj_qk")
    v = _matmul([(a, w_in16[:, o_qkv + 2 * aw:o_qi])], [], _epi_plain, aw, BF16, 1024, 1024, "proj_v")
    qkv = jnp.concatenate([qk, v], axis=1)

    cos_i, sin_i, half = _rope_tables(seq, IDX_ROPE_DIM, IDX_DIM)
    zeros = jnp.zeros((seq, half), F32)
    cos_i = jnp.concatenate([cos_i, cos_i, jnp.ones((seq, IDX_DIM - 2 * half), F32)], axis=1)
    sin_a = jnp.concatenate([-sin_i, zeros, jnp.zeros((seq, IDX_DIM - 2 * half), F32)], axis=1)
    sin_b = jnp.concatenate([zeros, sin_i, jnp.zeros((seq, IDX_DIM - 2 * half), F32)], axis=1)
    itabs = [tab(cos_i), tab(sin_a), tab(sin_b)]
    qi = _matmul([(a, w_in16[:, o_qi:o_ki])], itabs, _epi_rope_idx, n_idx * IDX_DIM, BF16, bm_t, 1024, "proj_qi")
    w_kiwi = jnp.zeros((d, 2 * IDX_DIM), BF16).at[:, :IDX_DIM + n_idx].set(w_in16[:, o_ki:o_gt])
    kiwi = _matmul([(a, w_kiwi)], itabs, _epi_kiwi, 2 * IDX_DIM, F32, bm_t, 2 * IDX_DIM, "proj_kiwi")

    gates = _matmul([(a, w_in16[:, o_gt:])],
                    [(b_gate.reshape(1, 2 * d), (1, _pick(1024, 2 * d)), lambda i, j: (0, j))],
                    _epi_sigmoid_bias, 2 * d, BF16, 1024, 1024, "proj_gates")

    g_ssm = _s5_branch(u, bsz, seq, lam_re, lam_im, log_dt, b_re, b_im, c_re, c_im, d_skip)
    bn_g = _pick(1024, ssm_w)
    y_ssm = _matmul([(g_ssm, w_glu.astype(BF16))], [(g_ssm, (_pick(1024, n_tok), bn_g), lambda i, j: (i, j))],
                    _epi_glu, ssm_w, BF16, 1024, 1024, "ssm_glu")

    y_att = _dsa_branch(qkv, qi, kiwi, bsz, seq, n_heads, n_idx)

    bm_m, bn_m = _pick(512, n_tok), _pick(1024, d)
    nj = d // bn_m
    merged = _matmul([(y_ssm, w_ssm_out.astype(BF16)), (y_att, w_attn_out.astype(BF16))],
                     [(gates, (bm_m, bn_m), lambda i, j: (i, j)), (gates, (bm_m, bn_m), lambda i, j: (i, nj + j))],
                     _epi_merge, d, BF16, bm_m, bn_m, "merge")
    h1 = _matmul([(merged, w_o.astype(BF16))], [(h, (bm_m, bn_m), lambda i, j: (i, j))],
                 _epi_residual, d, F32, bm_m, bn_m, "out_proj")

    y4 = _moe_branch(h1, g_moe, w_router, b_router, w1, b1, w2, b2)
    h2, n2 = _combine(h1, y4, g_ple)

    h3 = _matmul([(n2, w_ple_gate.astype(BF16)), (p_i.astype(BF16), w_ple.astype(BF16))],
                 [(h2, (bm_m, bn_m), lambda i, j: (i, j)), (b_ple_gate.reshape(1, d), (1, bn_m), lambda i, j: (0, j))],
                 _epi_ple, d, F32, bm_m, bn_m, "ple")
    return h3


def kernel(x, p, g_mix, w_in, b_gate, ssm_lambda_re, ssm_lambda_im, ssm_log_dt, ssm_b_re, ssm_b_im, ssm_c_re, ssm_c_im, ssm_d, ssm_w_glu, w_ssm_out, w_attn_out, w_o, g_moe, w_router, b_router, w_expert_in, b_expert_in, w_expert_out, b_expert_out, g_ple, w_ple, w_ple_gate, b_ple_gate, g_final):
    bsz, seq, d = x.shape
    h = x.reshape(bsz * seq, d)
    for i in range(p.shape[0]):
        h = _layer(h, p[i].reshape(bsz * seq, -1), g_mix[i], w_in[i], b_gate[i], ssm_lambda_re[i],
                   ssm_lambda_im[i], ssm_log_dt[i], ssm_b_re[i], ssm_b_im[i], ssm_c_re[i], ssm_c_im[i],
                   ssm_d[i], ssm_w_glu[i], w_ssm_out[i], w_attn_out[i], w_o[i], g_moe[i], w_router[i],
                   b_router[i], w_expert_in[i], b_expert_in[i], w_expert_out[i], b_expert_out[i],
                   g_ple[i], w_ple[i], w_ple_gate[i], b_ple_gate[i], bsz, seq)
    return _rmsnorm(h, g_final, F32, "rms_final").reshape(bsz, seq, d)
```

```python
import functools
import math

import jax
import jax.numpy as jnp
import numpy as np
from jax import lax
from jax.experimental import pallas as pl
from jax.experimental.pallas import tpu as pltpu

EPS = 1e-6
HEAD_DIM = 128
IDX_DIM = 128
IDX_ROPE_DIM = 64
TOPK_MAX = 256
ROPE_THETA = 10000.0
TOP_K = 4
SWIGLU_ALPHA = 1.702
SWIGLU_LIMIT = 7.0
LANES = 128
MXU_DIM = 256
VMEM_LIMIT = 56 * 1024 * 1024
NEG_BIG = -1e30
ATTN_HEADS_PER_STEP = 2
ATTN_KEY_CHUNK = 1024
ATTN_Q_BLOCK = 256
INDEX_Q_BLOCK = 256

F32 = jnp.float32
BF16 = jnp.bfloat16


def _pick(pref, n):
    b = min(pref, n)
    while n % b:
        b -= 1
    return b


def _params(sem):
    return pltpu.CompilerParams(dimension_semantics=sem, vmem_limit_bytes=VMEM_LIMIT)


def _rms_kernel(x_ref, g_ref, o_ref):
    x = x_ref[...]
    ms = jnp.mean(x * x, axis=-1, keepdims=True)
    o_ref[...] = (x * lax.rsqrt(ms + EPS) * g_ref[...]).astype(o_ref.dtype)


def _rmsnorm(x2d, g, out_dtype, name):
    m, d = x2d.shape
    bm = _pick(256, m)
    return pl.pallas_call(
        _rms_kernel,
        out_shape=jax.ShapeDtypeStruct((m, d), out_dtype),
        grid=(m // bm,),
        in_specs=[pl.BlockSpec((bm, d), lambda i: (i, 0)), pl.BlockSpec((1, d), lambda i: (0, 0))],
        out_specs=pl.BlockSpec((bm, d), lambda i: (i, 0)),
        compiler_params=_params(("parallel",)),
        name=name,
    )(x2d, g.reshape(1, d))


def _mm_kernel(*refs, n_lhs, n_extra, epilogue):
    lhs = refs[: 2 * n_lhs]
    extras = refs[2 * n_lhs: 2 * n_lhs + n_extra]
    o_ref = refs[2 * n_lhs + n_extra]
    accs = [jnp.dot(lhs[2 * t][...], lhs[2 * t + 1][...], preferred_element_type=F32) for t in range(n_lhs)]
    epilogue(accs, extras, o_ref)


def _matmul(pairs, extras, epilogue, n_out, out_dtype, bm, bn, name, head_major=False):
    m = pairs[0][0].shape[0]
    bm = _pick(bm, m)
    bn = _pick(bn, n_out)
    in_specs, args = [], []
    for a, w in pairs:
        k = a.shape[1]
        in_specs += [pl.BlockSpec((bm, k), lambda i, j: (i, 0)), pl.BlockSpec((k, bn), lambda i, j: (0, j))]
        args += [a, w]
    for arr, bshape, imap in extras:
        in_specs.append(pl.BlockSpec(bshape, imap))
        args.append(arr)
    if head_major:
        out_shape = jax.ShapeDtypeStruct((n_out // LANES, m, LANES), out_dtype)
        out_spec = pl.BlockSpec((bn // LANES, bm, LANES), lambda i, j: (j, i, 0))
    else:
        out_shape = jax.ShapeDtypeStruct((m, n_out), out_dtype)
        out_spec = pl.BlockSpec((bm, bn), lambda i, j: (i, j))
    return pl.pallas_call(
        functools.partial(_mm_kernel, n_lhs=len(pairs), n_extra=len(extras), epilogue=epilogue),
        out_shape=out_shape,
        grid=(m // bm, n_out // bn),
        in_specs=in_specs,
        out_specs=out_spec,
        compiler_params=_params(("parallel", "arbitrary")),
        name=name,
    )(*args)


def _epi_plain(accs, extras, o_ref):
    o_ref[...] = accs[0].astype(o_ref.dtype)


def _epi_rope_full(accs, extras, o_ref):
    cos, sin = extras[0][...], extras[1][...]
    acc = accs[0]
    for h in range(acc.shape[1] // HEAD_DIM):
        xh = acc[:, h * HEAD_DIM:(h + 1) * HEAD_DIM]
        o_ref[:, h * HEAD_DIM:(h + 1) * HEAD_DIM] = (
            xh * cos + pltpu.roll(xh, HEAD_DIM // 2, 1) * sin).astype(o_ref.dtype)


def _rope_idx(xh, cos, sin_a, sin_b):
    half = IDX_ROPE_DIM // 2
    return xh * cos + pltpu.roll(xh, IDX_DIM - half, 1) * sin_a + pltpu.roll(xh, half, 1) * sin_b


def _epi_rope_idx(accs, extras, o_ref):
    cos, sin_a, sin_b = extras[0][...], extras[1][...], extras[2][...]
    acc = accs[0]
    for h in range(acc.shape[1] // IDX_DIM):
        xh = acc[:, h * IDX_DIM:(h + 1) * IDX_DIM]
        o_ref[h] = _rope_idx(xh, cos, sin_a, sin_b).astype(o_ref.dtype)


def _epi_kiwi(accs, extras, o_ref):
    cos, sin_a, sin_b = extras[0][...], extras[1][...], extras[2][...]
    acc = accs[0]
    o_ref[:, :IDX_DIM] = _rope_idx(acc[:, :IDX_DIM], cos, sin_a, sin_b).astype(o_ref.dtype)
    o_ref[:, IDX_DIM:] = acc[:, IDX_DIM:].astype(o_ref.dtype)


def _epi_sigmoid_bias(accs, extras, o_ref):
    o_ref[...] = jax.nn.sigmoid(accs[0] + extras[0][...]).astype(o_ref.dtype)


def _epi_glu(accs, extras, o_ref):
    g = extras[0][...].astype(F32)
    o_ref[...] = (g * jax.nn.sigmoid(accs[0])).astype(o_ref.dtype)


def _epi_merge(accs, extras, o_ref):
    g0 = extras[0][...].astype(F32)
    g1 = extras[1][...].astype(F32)
    o_ref[...] = (g0 * accs[0] + g1 * accs[1]).astype(o_ref.dtype)


def _epi_residual(accs, extras, o_ref):
    o_ref[...] = (extras[0][...] + accs[0]).astype(o_ref.dtype)


def _epi_ple(accs, extras, o_ref):
    h, b = extras[0][...], extras[1][...]
    o_ref[...] = (h + accs[1] * jax.nn.sigmoid(accs[0] + b)).astype(o_ref.dtype)


def _s5_kernel(u_ref, w1_ref, w2_ref, w3_ref, ar_ref, ai_ref, d_ref, o_ref, *, n_chunks):
    hi = lax.Precision.HIGHEST
    u = u_ref[0]
    rows = u.shape[0]
    x = jnp.dot(u, w2_ref[0], precision=hi, preferred_element_type=F32)
    cidx = lax.broadcasted_iota(jnp.int32, (rows, 1), 0) % n_chunks
    half = x.shape[1] // 2
    for k in range(int(math.log2(n_chunks))):
        sh = 1 << k
        s = jnp.where(cidx >= sh, pltpu.roll(x, sh, 0), 0.0)
        x = x + ar_ref[0, k:k + 1, :] * s + ai_ref[0, k:k + 1, :] * pltpu.roll(s, half, 1)
    xprev = jnp.where(cidx >= 1, pltpu.roll(x, 1, 0), 0.0)
    y = (jnp.dot(u, w1_ref[0], precision=hi, preferred_element_type=F32)
         + jnp.dot(xprev, w3_ref[0], precision=hi, preferred_element_type=F32)
         + d_ref[0] * u)
    o_ref[0] = jax.nn.gelu(y).astype(o_ref.dtype)


def _s5_weights(lam_re, lam_im, log_dt, b_re, b_im, c_re, c_im, chunk, n_chunks):
    hi = lax.Precision.HIGHEST
    g, n, p = b_re.shape
    dt = jnp.exp(log_dt)[:, None]
    mag = jnp.exp(lam_re * dt)
    lb_re, lb_im = mag * jnp.cos(lam_im * dt), mag * jnp.sin(lam_im * dt)
    den = lam_re * lam_re + lam_im * lam_im
    nr, ni = lb_re - 1.0, lb_im
    f_re = (nr * lam_re + ni * lam_im) / den
    f_im = (ni * lam_re - nr * lam_im) / den
    bb_re = f_re[..., None] * b_re - f_im[..., None] * b_im
    bb_im = f_re[..., None] * b_im + f_im[..., None] * b_re

    def power(d):
        m = jnp.exp(lam_re * dt * d[:, None, None])
        ang = lam_im * dt * d[:, None, None]
        return m * jnp.cos(ang), m * jnp.sin(ang)

    pw_re, pw_im = power(jnp.arange(chunk + 1, dtype=F32))
    cp_re = c_re[None] * pw_re[:, :, None, :] - c_im[None] * pw_im[:, :, None, :]
    cp_im = c_re[None] * pw_im[:, :, None, :] + c_im[None] * pw_re[:, :, None, :]
    kd = (jnp.einsum('dgpn,gnj->dgpj', cp_re[:chunk], bb_re, precision=hi)
          - jnp.einsum('dgpn,gnj->dgpj', cp_im[:chunk], bb_im, precision=hi))
    s_idx = np.arange(chunk)[:, None]
    t_idx = np.arange(chunk)[None, :]
    lag = np.clip(t_idx - s_idx, 0, chunk - 1)
    w1 = jnp.where((t_idx >= s_idx)[:, :, None, None, None], kd[lag], 0.0)
    w1 = w1.transpose(2, 0, 4, 1, 3).reshape(g, chunk * p, chunk * p)
    rev_re, rev_im = pw_re[:chunk][::-1], pw_im[:chunk][::-1]
    z_re = rev_re[..., None] * bb_re[None] - rev_im[..., None] * bb_im[None]
    z_im = rev_re[..., None] * bb_im[None] + rev_im[..., None] * bb_re[None]
    w2 = jnp.concatenate([z_re, z_im], axis=2).transpose(1, 0, 3, 2).reshape(g, chunk * p, 2 * n)
    w3 = jnp.concatenate([cp_re[1:], -cp_im[1:]], axis=3)
    w3 = w3.transpose(1, 3, 0, 2).reshape(g, 2 * n, chunk * p)
    n_steps = int(math.log2(n_chunks))
    a_re, a_im = power(chunk * (2.0 ** jnp.arange(n_steps, dtype=F32)))
    ar = jnp.concatenate([a_re, a_re], axis=2).transpose(1, 0, 2)
    ai = jnp.concatenate([-a_im, a_im], axis=2).transpose(1, 0, 2)
    return w1, w2, w3, ar, ai


def _s5_branch(u2d, bsz, seq, lam_re, lam_im, log_dt, b_re, b_im, c_re, c_im, d_skip):
    g, n, p = b_re.shape
    chunk = MXU_DIM // p
    n_chunks = seq // chunk
    assert n_chunks & (n_chunks - 1) == 0 and 2 * n == LANES
    w1, w2, w3, ar, ai = _s5_weights(lam_re, lam_im, log_dt, b_re, b_im, c_re, c_im, chunk, n_chunks)
    n_steps = ar.shape[1]
    rows = bsz * n_chunks
    lp = chunk * p
    ug = u2d.reshape(bsz, n_chunks, chunk, g, p).transpose(3, 0, 1, 2, 4).reshape(g, rows, lp)
    dg = jnp.tile(d_skip.reshape(g, 1, p), (1, chunk, 1)).reshape(g, 1, lp)
    yg = pl.pallas_call(
        functools.partial(_s5_kernel, n_chunks=n_chunks),
        out_shape=jax.ShapeDtypeStruct((g, rows, lp), BF16),
        grid=(g,),
        in_specs=[
            pl.BlockSpec((1, rows, lp), lambda i: (i, 0, 0)),
            pl.BlockSpec((1, lp, lp), lambda i: (i, 0, 0)),
            pl.BlockSpec((1, lp, 2 * n), lambda i: (i, 0, 0)),
            pl.BlockSpec((1, 2 * n, lp), lambda i: (i, 0, 0)),
            pl.BlockSpec((1, n_steps, 2 * n), lambda i: (i, 0, 0)),
            pl.BlockSpec((1, n_steps, 2 * n), lambda i: (i, 0, 0)),
            pl.BlockSpec((1, 1, lp), lambda i: (i, 0, 0)),
        ],
        out_specs=pl.BlockSpec((1, rows, lp), lambda i: (i, 0, 0)),
        compiler_params=_params(("parallel",)),
        name="s5_scan",
    )(ug, w1, w2, w3, ar, ai, dg)
    return yg.reshape(g, bsz, n_chunks, chunk, p).transpose(1, 2, 3, 0, 4).reshape(bsz * seq, g * p)


def _dsa_index_kernel(qi_ref, ki_ref, wi_ref, o_ref, key_ref, *, n_heads, topk, scale, kc):
    qb, seq = o_ref.shape
    t0 = pl.program_id(1) * qb
    n_kc = (t0 + qb + kc - 1) // kc
    w = wi_ref[...] * scale
    wcols = [w[:, h:h + 1] for h in range(n_heads)]
    row = t0 + lax.broadcasted_iota(jnp.int32, (qb, 1), 0)

    def chunk_causal(c):
        return c * kc + lax.broadcasted_iota(jnp.int32, (1, kc), 1) <= row

    def scores(c, carry):
        off = pl.multiple_of(c * kc, kc)
        kchunk = ki_ref[pl.ds(off, kc), :]
        s = jnp.zeros((qb, kc), F32)
        for h in range(n_heads):
            z = lax.dot_general(qi_ref[h], kchunk, (((1,), (1,)), ((), ())), preferred_element_type=F32)
            s = s + wcols[h] * jnp.maximum(z, 0.0)
        s = jnp.where(chunk_causal(c), s, -jnp.inf)
        bits = lax.bitcast_convert_type(s, jnp.int32)
        key_ref[:, pl.ds(off, kc)] = jnp.where(bits < 0, bits ^ jnp.int32(0x7FFFFFFF), bits)
        return carry

    lax.fori_loop(0, n_kc, scores, 0)

    def bisect(it, thr):
        cand = thr + lax.shift_left(jnp.int32(1), jnp.int32(31) - jnp.asarray(it, jnp.int32))

        def count(c, acc):
            k = key_ref[:, pl.ds(pl.multiple_of(c * kc, kc), kc)]
            hit = jnp.where(k >= cand, 1.0, 0.0)
            for j in range(kc // LANES):
                acc = acc + hit[:, j * LANES:(j + 1) * LANES]
            return acc

        acc = lax.fori_loop(0, n_kc, count, jnp.zeros((qb, LANES), F32))
        cnt = jnp.sum(acc, axis=1, keepdims=True)
        return jnp.where(cnt >= topk, cand, thr)

    thr = lax.fori_loop(0, 32, bisect, jnp.full((qb, 1), jnp.iinfo(jnp.int32).min, jnp.int32))

    o_ref[...] = jnp.full(o_ref.shape, NEG_BIG, o_ref.dtype)

    def emit(c, carry):
        off = pl.multiple_of(c * kc, kc)
        k = key_ref[:, pl.ds(off, kc)]
        bias = jnp.where(k >= thr, jnp.where(chunk_causal(c), 0.0, NEG_BIG), NEG_BIG)
        o_ref[:, pl.ds(off, kc)] = bias.astype(o_ref.dtype)
        return carry

    lax.fori_loop(0, n_kc, emit, 0)


def _dsa_attn_kernel(q_ref, k_ref, v_ref, m_ref, o_ref, *, scale, kc):
    qa = q_ref.shape[0]
    n_h = q_ref.shape[1] // HEAD_DIM
    n_kc = ((pl.program_id(2) + 1) * qa + kc - 1) // kc
    qs = [(q_ref[:, h * HEAD_DIM:(h + 1) * HEAD_DIM].astype(F32) * (scale * math.log2(math.e))).astype(BF16)
          for h in range(n_h)]

    def chunk(c, carry):
        off = pl.multiple_of(c * kc, kc)
        bias = m_ref[:, pl.ds(off, kc)].astype(F32)
        out = []
        for h in range(n_h):
            m, l, acc = carry[h]
            hs = slice(h * HEAD_DIM, (h + 1) * HEAD_DIM)
            s = lax.dot_general(qs[h], k_ref[pl.ds(off, kc), hs], (((1,), (1,)), ((), ())),
                                preferred_element_type=F32) + bias
            m_new = jnp.maximum(m, jnp.max(s, axis=1, keepdims=True))
            alpha = jnp.exp2(m - m_new)
            p = jnp.exp2(s - m_new)
            l = alpha * l + jnp.sum(p, axis=1, keepdims=True)
            acc = alpha * acc + jnp.dot(p.astype(BF16), v_ref[pl.ds(off, kc), hs], preferred_element_type=F32)
            out.append((m_new, l, acc))
        return tuple(out)

    init = tuple((jnp.full((qa, 1), -jnp.inf, F32), jnp.zeros((qa, 1), F32), jnp.zeros((qa, HEAD_DIM), F32))
                 for _ in range(n_h))
    res = lax.fori_loop(0, n_kc, chunk, init)
    for h in range(n_h):
        _, l, acc = res[h]
        o_ref[:, h * HEAD_DIM:(h + 1) * HEAD_DIM] = (acc / l).astype(o_ref.dtype)


def _dsa_branch(qk, v, qi, kiwi, bsz, seq, n_heads, n_idx_heads):
    n_tok = bsz * seq
    aw = n_heads * HEAD_DIM
    topk = min(TOPK_MAX, seq // 4)
    qb = _pick(INDEX_Q_BLOCK, seq)
    nq = seq // qb
    mask = pl.pallas_call(
        functools.partial(_dsa_index_kernel, n_heads=n_idx_heads, topk=float(topk),
                          scale=(IDX_DIM ** -0.5) * (n_idx_heads ** -0.5), kc=_pick(MXU_DIM, seq)),
        out_shape=jax.ShapeDtypeStruct((n_tok, seq), BF16),
        grid=(bsz, nq),
        in_specs=[
            pl.BlockSpec((n_idx_heads, qb, IDX_DIM), lambda b, i: (0, b * nq + i, 0)),
            pl.BlockSpec((seq, IDX_DIM), lambda b, i: (b, 0)),
            pl.BlockSpec((qb, IDX_DIM), lambda b, i: (b * nq + i, 1)),
        ],
        out_specs=pl.BlockSpec((qb, seq), lambda b, i: (b * nq + i, 0)),
        scratch_shapes=[pltpu.VMEM((qb, seq), jnp.int32)],
        compiler_params=_params(("parallel", "arbitrary")),
        name="dsa_index",
    )(qi, kiwi.astype(BF16), kiwi)
    return _dsa_attend(qk, v, mask, bsz, seq, n_heads)


def _dsa_attend(qk, v, mask, bsz, seq, n_heads):
    n_tok = bsz * seq
    aw = n_heads * HEAD_DIM
    qa = _pick(ATTN_Q_BLOCK, seq)
    na = seq // qa
    hp = ATTN_HEADS_PER_STEP
    hw = hp * HEAD_DIM
    ng = n_heads // hp
    return pl.pallas_call(
        functools.partial(_dsa_attn_kernel, scale=HEAD_DIM ** -0.5, kc=_pick(ATTN_KEY_CHUNK, seq)),
        out_shape=jax.ShapeDtypeStruct((n_tok, aw), BF16),
        grid=(bsz, ng, na),
        in_specs=[
            pl.BlockSpec((qa, hw), lambda b, h, i: (b * na + i, h)),
            pl.BlockSpec((seq, hw), lambda b, h, i: (b, ng + h)),
            pl.BlockSpec((seq, hw), lambda b, h, i: (b, h)),
            pl.BlockSpec((qa, seq), lambda b, h, i: (b * na + i, 0)),
        ],
        out_specs=pl.BlockSpec((qa, hw), lambda b, h, i: (b * na + i, h)),
        compiler_params=_params(("parallel", "parallel", "arbitrary")),
        name="dsa_attn",
    )(qk, qk, v, mask)


def _router_kernel(h_ref, g_ref, wr_ref, br_ref, m_ref, r_ref, *, n_experts):
    x = h_ref[...]
    ms = jnp.mean(x * x, axis=-1, keepdims=True)
    m = x * lax.rsqrt(ms + EPS) * g_ref[...]
    m_ref[...] = m
    logits = jnp.dot(m, wr_ref[...], precision=lax.Precision.HIGHEST, preferred_element_type=F32) + br_ref[...]
    lane = lax.broadcasted_iota(jnp.int32, logits.shape, 1)
    l = jnp.where(lane < n_experts, logits, -jnp.inf)
    vals, idxs = [], []
    for _ in range(TOP_K):
        mx = jnp.max(l, axis=1, keepdims=True)
        am = jnp.min(jnp.where(l == mx, lane, LANES), axis=1, keepdims=True)
        vals.append(mx)
        idxs.append(am)
        l = jnp.where(lane == am, -jnp.inf, l)
    exps = [jnp.exp(v - vals[0]) for v in vals]
    tot = exps[0] + exps[1] + exps[2] + exps[3]
    out = jnp.zeros(logits.shape, F32)
    for k in range(TOP_K):
        out = jnp.where(lane == k, idxs[k].astype(F32), out)
        out = jnp.where(lane == TOP_K + k, exps[k] / tot, out)
    r_ref[...] = out


def _deinterleave_kernel(w_ref, p_ref, g_ref, l_ref):
    w = w_ref[0].astype(BF16)
    for c in range(w.shape[1] // MXU_DIM):
        z = jnp.dot(w[:, c * MXU_DIM:(c + 1) * MXU_DIM], p_ref[...], preferred_element_type=F32)
        g_ref[0, :, c * LANES:(c + 1) * LANES] = z[:, :LANES].astype(BF16)
        l_ref[0, :, c * LANES:(c + 1) * LANES] = z[:, LANES:].astype(BF16)


def _deinterleave(w1):
    n_exp, d, f2 = w1.shape
    tk = _pick(512, d)
    perm = np.zeros((MXU_DIM, MXU_DIM), np.float32)
    perm[np.arange(0, MXU_DIM, 2), np.arange(LANES)] = 1.0
    perm[np.arange(1, MXU_DIM, 2), LANES + np.arange(LANES)] = 1.0
    out = jax.ShapeDtypeStruct((n_exp, d, f2 // 2), BF16)
    return pl.pallas_call(
        _deinterleave_kernel,
        out_shape=(out, out),
        grid=(n_exp, d // tk),
        in_specs=[pl.BlockSpec((1, tk, f2), lambda e, k: (e, k, 0)),
                  pl.BlockSpec((MXU_DIM, MXU_DIM), lambda e, k: (0, 0))],
        out_specs=(pl.BlockSpec((1, tk, f2 // 2), lambda e, k: (e, k, 0)),
                   pl.BlockSpec((1, tk, f2 // 2), lambda e, k: (e, k, 0))),
        compiler_params=_params(("parallel", "parallel")),
        name="moe_deinterleave",
    )(w1, jnp.asarray(perm, BF16))


def _moe_kernel(blk_e_ref, nact_ref, nval_ref, tok_ref, tgt_ref, m_hbm, w1g_ref, w1l_ref, b1g_ref, b1l_ref,
                w2_ref, b2_ref, y_hbm, xbuf, xb, acc, sem_in, sem_out, *, blk, n_f):
    i = pl.program_id(0)
    f = pl.program_id(1)
    active = i < nact_ref[0]

    def row_in(r):
        return pltpu.make_async_copy(m_hbm.at[pl.ds(tok_ref[0, 0, r], 1)], xbuf.at[pl.ds(r, 1)], sem_in)

    def row_out(r):
        return pltpu.make_async_copy(acc.at[pl.ds(r, 1)], y_hbm.at[pl.ds(tgt_ref[0, 0, r], 1)], sem_out)

    @pl.when(jnp.logical_and(active, f == 0))
    def _gather():
        lax.fori_loop(0, blk, lambda r, c: (row_in(r).start(), c)[1], 0, unroll=8)
        lax.fori_loop(0, blk, lambda r, c: (row_in(r).wait(), c)[1], 0, unroll=8)
        xb[...] = xbuf[...].astype(BF16)

    @pl.when(active)
    def _compute():
        x = xb[...]
        hg = jnp.dot(x, w1g_ref[0], preferred_element_type=F32) + b1g_ref[0]
        hl = jnp.dot(x, w1l_ref[0], preferred_element_type=F32) + b1l_ref[0]
        x_glu = jnp.minimum(hg, SWIGLU_LIMIT)
        x_lin = jnp.clip(hl, -SWIGLU_LIMIT, SWIGLU_LIMIT)
        act = x_glu * jax.nn.sigmoid(SWIGLU_ALPHA * x_glu) * (x_lin + 1.0)
        contrib = jnp.dot(act.astype(BF16), w2_ref[0], preferred_element_type=F32)

        @pl.when(f == 0)
        def _():
            acc[...] = contrib + b2_ref[0]

        @pl.when(f > 0)
        def _():
            acc[...] += contrib

    @pl.when(jnp.logical_and(active, f == n_f - 1))
    def _scatter():
        n_valid = nval_ref[i]
        lax.fori_loop(0, n_valid, lambda r, c: (row_out(r).start(), c)[1], 0)
        lax.fori_loop(0, n_valid, lambda r, c: (row_out(r).wait(), c)[1], 0)


def _moe_branch(h1, g_moe, w_router, b_router, w1, b1, w2, b2):
    n_tok, d = h1.shape
    n_exp, _, f2 = w1.shape
    fdim = f2 // 2
    bm = _pick(256, n_tok)
    wr = jnp.zeros((d, LANES), F32).at[:, :n_exp].set(w_router)
    br = jnp.zeros((1, LANES), F32).at[0, :n_exp].set(b_router)
    m, route = pl.pallas_call(
        functools.partial(_router_kernel, n_experts=n_exp),
        out_shape=(jax.ShapeDtypeStruct((n_tok, d), F32), jax.ShapeDtypeStruct((n_tok, LANES), F32)),
        grid=(n_tok // bm,),
        in_specs=[pl.BlockSpec((bm, d), lambda i: (i, 0)), pl.BlockSpec((1, d), lambda i: (0, 0)),
                  pl.BlockSpec((d, LANES), lambda i: (0, 0)), pl.BlockSpec((1, LANES), lambda i: (0, 0))],
        out_specs=(pl.BlockSpec((bm, d), lambda i: (i, 0)), pl.BlockSpec((bm, LANES), lambda i: (i, 0))),
        compiler_params=_params(("parallel",)),
        name="moe_router",
    )(h1, g_moe.reshape(1, d), wr, br)
    top_e = route[:, :TOP_K].astype(jnp.int32)

    n_assign = n_tok * TOP_K
    blk = _pick(512, n_assign)
    n_blocks = n_assign // blk + n_exp
    n_rows = n_blocks * blk
    e_flat = top_e.reshape(-1)
    onehot = (e_flat[:, None] == jnp.arange(n_exp, dtype=jnp.int32)[None, :]).astype(jnp.int32)
    csum = jnp.cumsum(onehot, axis=0)
    rank = jnp.take_along_axis(csum, e_flat[:, None], axis=1)[:, 0] - 1
    counts = csum[-1]
    padded = (counts + blk - 1) // blk * blk
    pend = jnp.cumsum(padded)
    pstart = pend - padded
    dest = pstart[e_flat] + rank
    row_a = jnp.zeros((n_rows,), jnp.int32).at[dest].set(jnp.arange(1, n_assign + 1, dtype=jnp.int32)) - 1
    row_tok = jnp.maximum(row_a, 0) // TOP_K
    row_tgt = (jnp.maximum(row_a, 0) % TOP_K) * n_tok + row_tok
    n_act = (pend[-1] // blk).astype(jnp.int32)
    blk_ids = jnp.minimum(jnp.arange(n_blocks, dtype=jnp.int32), n_act - 1)
    blk_e = jnp.minimum(jnp.searchsorted(pend, blk_ids * blk, side='right'), n_exp - 1).astype(jnp.int32)
    n_valid = jnp.clip(pstart[blk_e] + counts[blk_e] - blk_ids * blk, 0, blk).astype(jnp.int32)

    fc = _pick(512, fdim)
    n_f = fdim // fc
    w1g, w1l = _deinterleave(w1)
    b1g = b1[:, 0::2].reshape(n_exp, 1, fdim)
    b1l = b1[:, 1::2].reshape(n_exp, 1, fdim)
    grid_spec = pltpu.PrefetchScalarGridSpec(
        num_scalar_prefetch=3,
        grid=(n_blocks, n_f),
        in_specs=[
            pl.BlockSpec((1, 1, blk), lambda i, f, be, na, nv: (i, 0, 0), memory_space=pltpu.SMEM),
            pl.BlockSpec((1, 1, blk), lambda i, f, be, na, nv: (i, 0, 0), memory_space=pltpu.SMEM),
            pl.BlockSpec(memory_space=pl.ANY),
            pl.BlockSpec((1, d, fc), lambda i, f, be, na, nv: (be[i], 0, f)),
            pl.BlockSpec((1, d, fc), lambda i, f, be, na, nv: (be[i], 0, f)),
            pl.BlockSpec((1, 1, fc), lambda i, f, be, na, nv: (be[i], 0, f)),
            pl.BlockSpec((1, 1, fc), lambda i, f, be, na, nv: (be[i], 0, f)),
            pl.BlockSpec((1, fc, d), lambda i, f, be, na, nv: (be[i], f, 0)),
            pl.BlockSpec((1, 1, d), lambda i, f, be, na, nv: (be[i], 0, 0)),
        ],
        out_specs=pl.BlockSpec(memory_space=pl.ANY),
        scratch_shapes=[pltpu.VMEM((blk, d), F32), pltpu.VMEM((blk, d), BF16), pltpu.VMEM((blk, d), F32),
                        pltpu.SemaphoreType.DMA, pltpu.SemaphoreType.DMA],
    )
    y4 = pl.pallas_call(
        functools.partial(_moe_kernel, blk=blk, n_f=n_f),
        out_shape=jax.ShapeDtypeStruct((n_assign, d), F32),
        grid_spec=grid_spec,
        compiler_params=pltpu.CompilerParams(dimension_semantics=("arbitrary", "arbitrary"),
                                             vmem_limit_bytes=VMEM_LIMIT, has_side_effects=True),
        name="moe_experts",
    )(blk_e, n_act.reshape(1), n_valid, row_tok.reshape(n_blocks, 1, blk), row_tgt.reshape(n_blocks, 1, blk),
      m, w1g, w1l, b1g, b1l, w2.astype(BF16), b2.reshape(n_exp, 1, d))
    return y4, route


def _combine_kernel(h_ref, y0_ref, y1_ref, y2_ref, y3_ref, r_ref, g_ref, h2_ref, n2_ref):
    r = r_ref[...]
    gk = [r[:, TOP_K + k:TOP_K + k + 1] for k in range(TOP_K)]
    h2 = h_ref[...] + ((gk[0] * y0_ref[...] + gk[1] * y1_ref[...]) + (gk[2] * y2_ref[...] + gk[3] * y3_ref[...]))
    h2_ref[...] = h2
    ms = jnp.mean(h2 * h2, axis=-1, keepdims=True)
    n2_ref[...] = (h2 * lax.rsqrt(ms + EPS) * g_ref[...]).astype(n2_ref.dtype)


def _combine(h1, y4, route, g_ple):
    n_tok, d = h1.shape
    bm = _pick(128, n_tok)
    nb = n_tok // bm
    yspec = [pl.BlockSpec((bm, d), functools.partial(lambda i, k: (k * nb + i, 0), k=k)) for k in range(TOP_K)]
    return pl.pallas_call(
        _combine_kernel,
        out_shape=(jax.ShapeDtypeStruct((n_tok, d), F32), jax.ShapeDtypeStruct((n_tok, d), BF16)),
        grid=(nb,),
        in_specs=[pl.BlockSpec((bm, d), lambda i: (i, 0))] + yspec
        + [pl.BlockSpec((bm, LANES), lambda i: (i, 0)), pl.BlockSpec((1, d), lambda i: (0, 0))],
        out_specs=(pl.BlockSpec((bm, d), lambda i: (i, 0)), pl.BlockSpec((bm, d), lambda i: (i, 0))),
        compiler_params=_params(("parallel",)),
        name="moe_combine",
    )(h1, y4, y4, y4, y4, route, g_ple.reshape(1, d))


def _rope_tables(seq, rot_dim, width):
    half = rot_dim // 2
    inv = ROPE_THETA ** (-jnp.arange(half, dtype=F32) / half)
    ang = jnp.arange(seq, dtype=F32)[:, None] * inv[None, :]
    return jnp.cos(ang), jnp.sin(ang), half


def _layer(h, p_i, g_mix, w_in, b_gate, lam_re, lam_im, log_dt, b_re, b_im, c_re, c_im, d_skip, w_glu,
           w_ssm_out, w_attn_out, w_o, g_moe, w_router, b_router, w1, b1, w2, b2, g_ple, w_ple,
           w_ple_gate, b_ple_gate, bsz, seq):
    n_tok, d = h.shape
    ssm_w = w_glu.shape[0]
    aw = w_attn_out.shape[0]
    n_heads = aw // HEAD_DIM
    n_idx = (w_in.shape[1] - ssm_w - 3 * aw - IDX_DIM - 2 * d) // (IDX_DIM + 1)
    o_u, o_qkv, o_qi, o_ki, o_wi, o_gt = np.cumsum([0, ssm_w, 3 * aw, n_idx * IDX_DIM, IDX_DIM, n_idx]).tolist()
    bm_t = _pick(1024, seq)
    nbt = seq // bm_t

    a = _rmsnorm(h, g_mix, BF16, "rms_mix")
    w_in16 = w_in.astype(BF16)

    u = _matmul([(a, w_in16[:, o_u:o_qkv])], [], _epi_plain, ssm_w, F32, 1024, 1024, "proj_u")

    cos, sin, _ = _rope_tables(seq, HEAD_DIM, HEAD_DIM)
    cos_f = jnp.concatenate([cos, cos], axis=1)
    sin_f = jnp.concatenate([-sin, sin], axis=1)
    tab = lambda t: (t, (bm_t, LANES), lambda i, j: (i % nbt, 0))
    qk = _matmul([(a, w_in16[:, o_qkv:o_qkv + 2 * aw])], [tab(cos_f), tab(sin_f)], _epi_rope_full,
                 2 * aw, BF16, bm_t, 1024, "proj_qk")
    v = _matmul([(a, w_in16[:, o_qkv + 2 * aw:o_qi])], [], _epi_plain, aw, BF16, 1024, 1024, "proj_v")

    cos_i, sin_i, half = _rope_tables(seq, IDX_ROPE_DIM, IDX_DIM)
    zeros = jnp.zeros((seq, half), F32)
    cos_i = jnp.concatenate([cos_i, cos_i, jnp.ones((seq, IDX_DIM - 2 * half), F32)], axis=1)
    sin_a = jnp.concatenate([-sin_i, zeros, jnp.zeros((seq, IDX_DIM - 2 * half), F32)], axis=1)
    sin_b = jnp.concatenate([zeros, sin_i, jnp.zeros((seq, IDX_DIM - 2 * half), F32)], axis=1)
    itabs = [tab(cos_i), tab(sin_a), tab(sin_b)]
    qi = _matmul([(a, w_in16[:, o_qi:o_ki])], itabs, _epi_rope_idx, n_idx * IDX_DIM, BF16, bm_t, 1024, "proj_qi",
                 head_major=True)
    w_kiwi = jnp.zeros((d, 2 * IDX_DIM), BF16).at[:, :IDX_DIM + n_idx].set(w_in16[:, o_ki:o_gt])
    kiwi = _matmul([(a, w_kiwi)], itabs, _epi_kiwi, 2 * IDX_DIM, F32, bm_t, 2 * IDX_DIM, "proj_kiwi")

    gates = _matmul([(a, w_in16[:, o_gt:])],
                    [(b_gate.reshape(1, 2 * d), (1, _pick(1024, 2 * d)), lambda i, j: (0, j))],
                    _epi_sigmoid_bias, 2 * d, BF16, 1024, 1024, "proj_gates")

    g_ssm = _s5_branch(u, bsz, seq, lam_re, lam_im, log_dt, b_re, b_im, c_re, c_im, d_skip)
    bn_g = _pick(1024, ssm_w)
    y_ssm = _matmul([(g_ssm, w_glu.astype(BF16))], [(g_ssm, (_pick(1024, n_tok), bn_g), lambda i, j: (i, j))],
                    _epi_glu, ssm_w, BF16, 1024, 1024, "ssm_glu")

    y_att = _dsa_branch(qk, v, qi, kiwi, bsz, seq, n_heads, n_idx)

    bm_m, bn_m = _pick(512, n_tok), _pick(1024, d)
    nj = d // bn_m
    merged = _matmul([(y_ssm, w_ssm_out.astype(BF16)), (y_att, w_attn_out.astype(BF16))],
                     [(gates, (bm_m, bn_m), lambda i, j: (i, j)), (gates, (bm_m, bn_m), lambda i, j: (i, nj + j))],
                     _epi_merge, d, BF16, bm_m, bn_m, "merge")
    h1 = _matmul([(merged, w_o.astype(BF16))], [(h, (bm_m, bn_m), lambda i, j: (i, j))],
                 _epi_residual, d, F32, bm_m, bn_m, "out_proj")

    y4, route = _moe_branch(h1, g_moe, w_router, b_router, w1, b1, w2, b2)
    h2, n2 = _combine(h1, y4, route, g_ple)

    h3 = _matmul([(n2, w_ple_gate.astype(BF16)), (p_i.astype(BF16), w_ple.astype(BF16))],
                 [(h2, (bm_m, bn_m), lambda i, j: (i, j)), (b_ple_gate.reshape(1, d), (1, bn_m), lambda i, j: (0, j))],
                 _epi_ple, d, F32, bm_m, bn_m, "ple")
    return h3


def kernel(x, p, g_mix, w_in, b_gate, ssm_lambda_re, ssm_lambda_im, ssm_log_dt, ssm_b_re, ssm_b_im, ssm_c_re, ssm_c_im, ssm_d, ssm_w_glu, w_ssm_out, w_attn_out, w_o, g_moe, w_router, b_router, w_expert_in, b_expert_in, w_expert_out, b_expert_out, g_ple, w_ple, w_ple_gate, b_ple_gate, g_final):
    bsz, seq, d = x.shape
    h = x.reshape(bsz * seq, d)
    for i in range(p.shape[0]):
        h = _layer(h, p[i].reshape(bsz * seq, -1), g_mix[i], w_in[i], b_gate[i], ssm_lambda_re[i],
                   ssm_lambda_im[i], ssm_log_dt[i], ssm_b_re[i], ssm_b_im[i], ssm_c_re[i], ssm_c_im[i],
                   ssm_d[i], ssm_w_glu[i], w_ssm_out[i], w_attn_out[i], w_o[i], g_moe[i], w_router[i],
                   b_router[i], w_expert_in[i], b_expert_in[i], w_expert_out[i], b_expert_out[i],
                   g_ple[i], w_ple[i], w_ple_gate[i], b_ple_gate[i], bsz, seq)
    return _rmsnorm(h, g_final, F32, "rms_final").reshape(bsz, seq, d)
```

```python
import functools
import math

import jax
import jax.numpy as jnp
import numpy as np
from jax import lax
from jax.experimental import pallas as pl
from jax.experimental.pallas import tpu as pltpu

EPS = 1e-6
HEAD_DIM = 128
IDX_DIM = 128
IDX_ROPE_DIM = 64
TOPK_MAX = 256
ROPE_THETA = 10000.0
TOP_K = 4
SWIGLU_ALPHA = 1.702
SWIGLU_LIMIT = 7.0
LANES = 128
MXU_DIM = 256
VMEM_LIMIT = 56 * 1024 * 1024
NEG_BIG = -1e30
ATTN_HEADS_PER_STEP = 2
ATTN_KEY_CHUNK = 1024
ATTN_Q_BLOCK = 512
INDEX_Q_BLOCK = 256
S5_BATCH_BLOCK = 2

F32 = jnp.float32
BF16 = jnp.bfloat16


def _pick(pref, n):
    b = min(pref, n)
    while n % b:
        b -= 1
    return b


def _params(sem):
    return pltpu.CompilerParams(dimension_semantics=sem, vmem_limit_bytes=VMEM_LIMIT)


def _rms_kernel(x_ref, g_ref, o_ref):
    x = x_ref[...]
    ms = jnp.mean(x * x, axis=-1, keepdims=True)
    o_ref[...] = (x * lax.rsqrt(ms + EPS) * g_ref[...]).astype(o_ref.dtype)


def _rmsnorm(x2d, g, out_dtype, name):
    m, d = x2d.shape
    bm = _pick(256, m)
    return pl.pallas_call(
        _rms_kernel,
        out_shape=jax.ShapeDtypeStruct((m, d), out_dtype),
        grid=(m // bm,),
        in_specs=[pl.BlockSpec((bm, d), lambda i: (i, 0)), pl.BlockSpec((1, d), lambda i: (0, 0))],
        out_specs=pl.BlockSpec((bm, d), lambda i: (i, 0)),
        compiler_params=_params(("parallel",)),
        name=name,
    )(x2d, g.reshape(1, d))


def _mm_kernel(*refs, n_lhs, n_extra, epilogue):
    lhs = refs[: 2 * n_lhs]
    extras = refs[2 * n_lhs: 2 * n_lhs + n_extra]
    o_ref = refs[2 * n_lhs + n_extra]
    accs = [jnp.dot(lhs[2 * t][...], lhs[2 * t + 1][...], preferred_element_type=F32) for t in range(n_lhs)]
    epilogue(accs, extras, o_ref)


def _matmul(pairs, extras, epilogue, n_out, out_dtype, bm, bn, name, head_major=False):
    m = pairs[0][0].shape[0]
    bm = _pick(bm, m)
    bn = _pick(bn, n_out)
    in_specs, args = [], []
    for pair in pairs:
        a, w = pair[:2]
        off = pair[2] if len(pair) > 2 else 0
        if off % bn:
            w, off = w[:, off:off + n_out], 0
        k = a.shape[1]
        in_specs += [pl.BlockSpec((bm, k), lambda i, j: (i, 0)),
                     pl.BlockSpec((k, bn), functools.partial(lambda i, j, ob: (0, j + ob), ob=off // bn))]
        args += [a, w]
    for arr, bshape, imap in extras:
        in_specs.append(pl.BlockSpec(bshape, imap))
        args.append(arr)
    if head_major:
        out_shape = jax.ShapeDtypeStruct((n_out // LANES, m, LANES), out_dtype)
        out_spec = pl.BlockSpec((bn // LANES, bm, LANES), lambda i, j: (j, i, 0))
    else:
        out_shape = jax.ShapeDtypeStruct((m, n_out), out_dtype)
        out_spec = pl.BlockSpec((bm, bn), lambda i, j: (i, j))
    return pl.pallas_call(
        functools.partial(_mm_kernel, n_lhs=len(pairs), n_extra=len(extras), epilogue=epilogue),
        out_shape=out_shape,
        grid=(m // bm, n_out // bn),
        in_specs=in_specs,
        out_specs=out_spec,
        compiler_params=_params(("parallel", "arbitrary")),
        name=name,
    )(*args)


def _epi_plain(accs, extras, o_ref):
    o_ref[...] = accs[0].astype(o_ref.dtype)


def _epi_rope_full(accs, extras, o_ref):
    cos, sin = extras[0][...], extras[1][...]
    acc = accs[0]
    for h in range(acc.shape[1] // HEAD_DIM):
        xh = acc[:, h * HEAD_DIM:(h + 1) * HEAD_DIM]
        o_ref[:, h * HEAD_DIM:(h + 1) * HEAD_DIM] = (
            xh * cos + pltpu.roll(xh, HEAD_DIM // 2, 1) * sin).astype(o_ref.dtype)


def _rope_idx(xh, cos, sin_a, sin_b):
    half = IDX_ROPE_DIM // 2
    return xh * cos + pltpu.roll(xh, IDX_DIM - half, 1) * sin_a + pltpu.roll(xh, half, 1) * sin_b


def _epi_rope_idx(accs, extras, o_ref):
    cos, sin_a, sin_b = extras[0][...], extras[1][...], extras[2][...]
    acc = accs[0]
    for h in range(acc.shape[1] // IDX_DIM):
        xh = acc[:, h * IDX_DIM:(h + 1) * IDX_DIM]
        o_ref[h] = _rope_idx(xh, cos, sin_a, sin_b).astype(o_ref.dtype)


def _epi_kiwi(accs, extras, o_ref):
    cos, sin_a, sin_b = extras[0][...], extras[1][...], extras[2][...]
    acc = accs[0]
    o_ref[:, :IDX_DIM] = _rope_idx(acc[:, :IDX_DIM], cos, sin_a, sin_b).astype(o_ref.dtype)
    o_ref[:, IDX_DIM:] = acc[:, IDX_DIM:].astype(o_ref.dtype)


def _epi_sigmoid_bias(accs, extras, o_ref):
    o_ref[...] = jax.nn.sigmoid(accs[0] + extras[0][...]).astype(o_ref.dtype)


def _epi_glu(accs, extras, o_ref):
    g = extras[0][...].astype(F32)
    o_ref[...] = (g * jax.nn.sigmoid(accs[0])).astype(o_ref.dtype)


def _epi_merge(accs, extras, o_ref):
    g0 = extras[0][...].astype(F32)
    g1 = extras[1][...].astype(F32)
    o_ref[...] = (g0 * accs[0] + g1 * accs[1]).astype(o_ref.dtype)


def _epi_residual(accs, extras, o_ref):
    o_ref[...] = (extras[0][...] + accs[0]).astype(o_ref.dtype)


def _epi_ple(accs, extras, o_ref):
    h, b = extras[0][...], extras[1][...]
    o_ref[...] = (h + accs[1] * jax.nn.sigmoid(accs[0] + b)).astype(o_ref.dtype)


def _s5_kernel(u_ref, w1_ref, w2_ref, w3_ref, ar_ref, ai_ref, d_ref, o_ref, y_scr, *, n_chunks, chunk):
    rows = u_ref.shape[0] // chunk
    us = [u_ref[pl.ds(s, rows, stride=chunk), :] for s in range(chunk)]
    u = jnp.concatenate(us, axis=1).astype(BF16)
    x = jnp.dot(u, w2_ref[0], preferred_element_type=F32)
    cidx = lax.broadcasted_iota(jnp.int32, (rows, 1), 0) % n_chunks
    half = x.shape[1] // 2
    for k in range(int(math.log2(n_chunks))):
        sh = 1 << k
        s = jnp.where(cidx >= sh, pltpu.roll(x, sh, 0), 0.0)
        x = x + ar_ref[0, k:k + 1, :] * s + ai_ref[0, k:k + 1, :] * pltpu.roll(s, half, 1)
    xprev = jnp.where(cidx >= 1, pltpu.roll(x, 1, 0), 0.0)
    y = (jnp.dot(u, w1_ref[0], preferred_element_type=F32)
         + jnp.dot(xprev.astype(BF16), w3_ref[0], preferred_element_type=F32))
    d = d_ref[...]
    for s in range(chunk):
        y_scr[pl.ds(s, rows, stride=chunk), :] = jax.nn.gelu(y[:, s * LANES:(s + 1) * LANES] + d * us[s])
    o_ref[...] = y_scr[...].astype(o_ref.dtype)


def _s5_weights(lam_re, lam_im, log_dt, b_re, b_im, c_re, c_im, chunk, n_chunks):
    hi = lax.Precision.HIGHEST
    g, n, p = b_re.shape
    dt = jnp.exp(log_dt)[:, None]
    mag = jnp.exp(lam_re * dt)
    lb_re, lb_im = mag * jnp.cos(lam_im * dt), mag * jnp.sin(lam_im * dt)
    den = lam_re * lam_re + lam_im * lam_im
    nr, ni = lb_re - 1.0, lb_im
    f_re = (nr * lam_re + ni * lam_im) / den
    f_im = (ni * lam_re - nr * lam_im) / den
    bb_re = f_re[..., None] * b_re - f_im[..., None] * b_im
    bb_im = f_re[..., None] * b_im + f_im[..., None] * b_re

    def power(d):
        m = jnp.exp(lam_re * dt * d[:, None, None])
        ang = lam_im * dt * d[:, None, None]
        return m * jnp.cos(ang), m * jnp.sin(ang)

    pw_re, pw_im = power(jnp.arange(chunk + 1, dtype=F32))
    cp_re = c_re[None] * pw_re[:, :, None, :] - c_im[None] * pw_im[:, :, None, :]
    cp_im = c_re[None] * pw_im[:, :, None, :] + c_im[None] * pw_re[:, :, None, :]
    kd = (jnp.einsum('dgpn,gnj->dgpj', cp_re[:chunk], bb_re, precision=hi)
          - jnp.einsum('dgpn,gnj->dgpj', cp_im[:chunk], bb_im, precision=hi))
    s_idx = np.arange(chunk)[:, None]
    t_idx = np.arange(chunk)[None, :]
    lag = np.clip(t_idx - s_idx, 0, chunk - 1)
    w1 = jnp.where((t_idx >= s_idx)[:, :, None, None, None], kd[lag], 0.0)
    w1 = w1.transpose(2, 0, 4, 1, 3).reshape(g, chunk * p, chunk * p)
    rev_re, rev_im = pw_re[:chunk][::-1], pw_im[:chunk][::-1]
    z_re = rev_re[..., None] * bb_re[None] - rev_im[..., None] * bb_im[None]
    z_im = rev_re[..., None] * bb_im[None] + rev_im[..., None] * bb_re[None]
    w2 = jnp.concatenate([z_re, z_im], axis=2).transpose(1, 0, 3, 2).reshape(g, chunk * p, 2 * n)
    w3 = jnp.concatenate([cp_re[1:], -cp_im[1:]], axis=3)
    w3 = w3.transpose(1, 3, 0, 2).reshape(g, 2 * n, chunk * p)
    n_steps = int(math.log2(n_chunks))
    a_re, a_im = power(chunk * (2.0 ** jnp.arange(n_steps, dtype=F32)))
    ar = jnp.concatenate([a_re, a_re], axis=2).transpose(1, 0, 2)
    ai = jnp.concatenate([-a_im, a_im], axis=2).transpose(1, 0, 2)
    gpb = LANES // p
    sg = g // gpb
    eye = jnp.eye(gpb, dtype=F32)
    w1b = jnp.einsum('agsjtq,gh->asgjthq', w1.reshape(sg, gpb, chunk, p, chunk, p), eye)
    w1b = w1b.reshape(sg, chunk * LANES, chunk * LANES).astype(BF16)
    w2b = jnp.einsum('agsjrn,gh->asgjrhn', w2.reshape(sg, gpb, chunk, p, 2, n), eye)
    w2b = w2b.reshape(sg, chunk * LANES, 2 * gpb * n).astype(BF16)
    w3b = jnp.einsum('agrntq,gh->argnthq', w3.reshape(sg, gpb, 2, n, chunk, p), eye)
    w3b = w3b.reshape(sg, 2 * gpb * n, chunk * LANES).astype(BF16)
    arb = ar.reshape(sg, gpb, n_steps, 2, n).transpose(0, 2, 3, 1, 4).reshape(sg, n_steps, 2 * gpb * n)
    aib = ai.reshape(sg, gpb, n_steps, 2, n).transpose(0, 2, 3, 1, 4).reshape(sg, n_steps, 2 * gpb * n)
    return w1b, w2b, w3b, arb, aib


def _s5_branch(u2d, bsz, seq, lam_re, lam_im, log_dt, b_re, b_im, c_re, c_im, d_skip):
    g, n, p = b_re.shape
    chunk = MXU_DIM // p
    n_chunks = seq // chunk
    assert n_chunks & (n_chunks - 1) == 0 and 2 * n == LANES
    w1, w2, w3, ar, ai = _s5_weights(lam_re, lam_im, log_dt, b_re, b_im, c_re, c_im, chunk, n_chunks)
    n_steps = ar.shape[1]
    gpb = LANES // p
    assert g % gpb == 0
    n_tok = bsz * seq
    bb = _pick(S5_BATCH_BLOCK, bsz)
    trows = bb * seq
    lw, sw = chunk * LANES, 2 * gpb * n
    once = pl.Buffered(1)
    return pl.pallas_call(
        functools.partial(_s5_kernel, n_chunks=n_chunks, chunk=chunk),
        out_shape=jax.ShapeDtypeStruct((n_tok, g * p), BF16),
        grid=(g // gpb, bsz // bb),
        in_specs=[
            pl.BlockSpec((trows, LANES), lambda i, b: (b, i)),
            pl.BlockSpec((1, lw, lw), lambda i, b: (i, 0, 0), pipeline_mode=once),
            pl.BlockSpec((1, lw, sw), lambda i, b: (i, 0, 0), pipeline_mode=once),
            pl.BlockSpec((1, sw, lw), lambda i, b: (i, 0, 0), pipeline_mode=once),
            pl.BlockSpec((1, n_steps, sw), lambda i, b: (i, 0, 0)),
            pl.BlockSpec((1, n_steps, sw), lambda i, b: (i, 0, 0)),
            pl.BlockSpec((1, LANES), lambda i, b: (0, i)),
        ],
        out_specs=pl.BlockSpec((trows, LANES), lambda i, b: (b, i)),
        scratch_shapes=[pltpu.VMEM((trows, LANES), F32)],
        compiler_params=_params(("parallel", "arbitrary")),
        name="s5_scan",
    )(u2d, w1, w2, w3, ar, ai, d_skip.reshape(1, g * p))


def _dsa_index_kernel(qi_ref, ki_ref, wi_ref, o_ref, key_ref, *, n_heads, topk, scale, kc):
    qb, seq = o_ref.shape
    t0 = pl.program_id(1) * qb
    n_kc = (t0 + qb + kc - 1) // kc
    w = wi_ref[...] * scale
    wcols = [w[:, h:h + 1] for h in range(n_heads)]
    row = t0 + lax.broadcasted_iota(jnp.int32, (qb, 1), 0)

    def chunk_causal(c):
        return c * kc + lax.broadcasted_iota(jnp.int32, (1, kc), 1) <= row

    def scores(c, carry):
        off = pl.multiple_of(c * kc, kc)
        kchunk = ki_ref[pl.ds(off, kc), :]
        s = jnp.zeros((qb, kc), F32)
        for h in range(n_heads):
            z = lax.dot_general(qi_ref[h], kchunk, (((1,), (1,)), ((), ())), preferred_element_type=F32)
            s = s + wcols[h] * jnp.maximum(z, 0.0)
        s = jnp.where(chunk_causal(c), s, -jnp.inf)
        bits = lax.bitcast_convert_type(s, jnp.int32)
        key_ref[:, pl.ds(off, kc)] = jnp.where(bits < 0, bits ^ jnp.int32(0x7FFFFFFF), bits)
        return carry

    lax.fori_loop(0, n_kc, scores, 0)

    def bisect(it, thr):
        cand = thr + lax.shift_left(jnp.int32(1), jnp.int32(31) - jnp.asarray(it, jnp.int32))

        def count(c, acc):
            k = key_ref[:, pl.ds(pl.multiple_of(c * kc, kc), kc)]
            hit = jnp.where(k >= cand, 1.0, 0.0)
            for j in range(kc // LANES):
                acc = acc + hit[:, j * LANES:(j + 1) * LANES]
            return acc

        acc = lax.fori_loop(0, n_kc, count, jnp.zeros((qb, LANES), F32))
        cnt = jnp.sum(acc, axis=1, keepdims=True)
        return jnp.where(cnt >= topk, cand, thr)

    thr = lax.fori_loop(0, 32, bisect, jnp.full((qb, 1), jnp.iinfo(jnp.int32).min, jnp.int32))

    o_ref[...] = jnp.full(o_ref.shape, NEG_BIG, o_ref.dtype)

    def emit(c, carry):
        off = pl.multiple_of(c * kc, kc)
        k = key_ref[:, pl.ds(off, kc)]
        bias = jnp.where(k >= thr, jnp.where(chunk_causal(c), 0.0, NEG_BIG), NEG_BIG)
        o_ref[:, pl.ds(off, kc)] = bias.astype(o_ref.dtype)
        return carry

    lax.fori_loop(0, n_kc, emit, 0)


def _dsa_attn_kernel(q_ref, k_ref, v_ref, m_ref, o_ref, *, scale, kc):
    qa = q_ref.shape[0]
    n_h = q_ref.shape[1] // HEAD_DIM
    n_kc = ((pl.program_id(2) + 1) * qa + kc - 1) // kc
    qs = [(q_ref[:, h * HEAD_DIM:(h + 1) * HEAD_DIM].astype(F32) * (scale * math.log2(math.e))).astype(BF16)
          for h in range(n_h)]

    def chunk(c, carry):
        off = pl.multiple_of(c * kc, kc)
        bias = m_ref[:, pl.ds(off, kc)].astype(F32)
        out = []
        for h in range(n_h):
            m, l, acc = carry[h]
            hs = slice(h * HEAD_DIM, (h + 1) * HEAD_DIM)
            s = lax.dot_general(qs[h], k_ref[pl.ds(off, kc), hs], (((1,), (1,)), ((), ())),
                                preferred_element_type=F32) + bias
            m_new = jnp.maximum(m, jnp.max(s, axis=1, keepdims=True))
            alpha = jnp.exp2(m - m_new)
            p = jnp.exp2(s - m_new)
            l = alpha * l + jnp.sum(p, axis=1, keepdims=True)
            acc = alpha * acc + jnp.dot(p.astype(BF16), v_ref[pl.ds(off, kc), hs], preferred_element_type=F32)
            out.append((m_new, l, acc))
        return tuple(out)

    init = tuple((jnp.full((qa, 1), -jnp.inf, F32), jnp.zeros((qa, 1), F32), jnp.zeros((qa, HEAD_DIM), F32))
                 for _ in range(n_h))
    res = lax.fori_loop(0, n_kc, chunk, init)
    for h in range(n_h):
        _, l, acc = res[h]
        o_ref[:, h * HEAD_DIM:(h + 1) * HEAD_DIM] = (acc / l).astype(o_ref.dtype)


def _dsa_branch(qk, v, qi, kiwi, bsz, seq, n_heads, n_idx_heads):
    n_tok = bsz * seq
    aw = n_heads * HEAD_DIM
    topk = min(TOPK_MAX, seq // 4)
    qb = _pick(INDEX_Q_BLOCK, seq)
    nq = seq // qb
    mask = pl.pallas_call(
        functools.partial(_dsa_index_kernel, n_heads=n_idx_heads, topk=float(topk),
                          scale=(IDX_DIM ** -0.5) * (n_idx_heads ** -0.5), kc=_pick(MXU_DIM, seq)),
        out_shape=jax.ShapeDtypeStruct((n_tok, seq), BF16),
        grid=(bsz, nq),
        in_specs=[
            pl.BlockSpec((n_idx_heads, qb, IDX_DIM), lambda b, i: (0, b * nq + i, 0)),
            pl.BlockSpec((seq, IDX_DIM), lambda b, i: (b, 0)),
            pl.BlockSpec((qb, IDX_DIM), lambda b, i: (b * nq + i, 1)),
        ],
        out_specs=pl.BlockSpec((qb, seq), lambda b, i: (b * nq + i, 0)),
        scratch_shapes=[pltpu.VMEM((qb, seq), jnp.int32)],
        compiler_params=_params(("parallel", "arbitrary")),
        name="dsa_index",
    )(qi, kiwi.astype(BF16), kiwi)
    return _dsa_attend(qk, v, mask, bsz, seq, n_heads)


def _dsa_attend(qk, v, mask, bsz, seq, n_heads):
    n_tok = bsz * seq
    aw = n_heads * HEAD_DIM
    qa = _pick(ATTN_Q_BLOCK, seq)
    na = seq // qa
    hp = ATTN_HEADS_PER_STEP
    hw = hp * HEAD_DIM
    ng = n_heads // hp
    return pl.pallas_call(
        functools.partial(_dsa_attn_kernel, scale=HEAD_DIM ** -0.5, kc=_pick(ATTN_KEY_CHUNK, seq)),
        out_shape=jax.ShapeDtypeStruct((n_tok, aw), BF16),
        grid=(bsz, ng, na),
        in_specs=[
            pl.BlockSpec((qa, hw), lambda b, h, i: (b * na + i, h)),
            pl.BlockSpec((seq, hw), lambda b, h, i: (b, ng + h)),
            pl.BlockSpec((seq, hw), lambda b, h, i: (b, h)),
            pl.BlockSpec((qa, seq), lambda b, h, i: (b * na + i, 0)),
        ],
        out_specs=pl.BlockSpec((qa, hw), lambda b, h, i: (b * na + i, h)),
        compiler_params=_params(("parallel", "parallel", "arbitrary")),
        name="dsa_attn",
    )(qk, qk, v, mask)


def _router_kernel(h_ref, g_ref, wr_ref, br_ref, m_ref, r_ref, *, n_experts):
    x = h_ref[...]
    ms = jnp.mean(x * x, axis=-1, keepdims=True)
    m = x * lax.rsqrt(ms + EPS) * g_ref[...]
    m_ref[...] = m
    logits = jnp.dot(m, wr_ref[...], precision=lax.Precision.HIGHEST, preferred_element_type=F32) + br_ref[...]
    lane = lax.broadcasted_iota(jnp.int32, logits.shape, 1)
    l = jnp.where(lane < n_experts, logits, -jnp.inf)
    vals, idxs = [], []
    for _ in range(TOP_K):
        mx = jnp.max(l, axis=1, keepdims=True)
        am = jnp.min(jnp.where(l == mx, lane, LANES), axis=1, keepdims=True)
        vals.append(mx)
        idxs.append(am)
        l = jnp.where(lane == am, -jnp.inf, l)
    exps = [jnp.exp(v - vals[0]) for v in vals]
    tot = exps[0] + exps[1] + exps[2] + exps[3]
    out = jnp.zeros(logits.shape, F32)
    for k in range(TOP_K):
        out = jnp.where(lane == k, idxs[k].astype(F32), out)
        out = jnp.where(lane == TOP_K + k, exps[k] / tot, out)
    r_ref[...] = out


def _deinterleave_kernel(w_ref, p_ref, g_ref, l_ref):
    w = w_ref[0].astype(BF16)
    for c in range(w.shape[1] // MXU_DIM):
        z = jnp.dot(w[:, c * MXU_DIM:(c + 1) * MXU_DIM], p_ref[...], preferred_element_type=F32)
        g_ref[0, :, c * LANES:(c + 1) * LANES] = z[:, :LANES].astype(BF16)
        l_ref[0, :, c * LANES:(c + 1) * LANES] = z[:, LANES:].astype(BF16)


def _deinterleave(w1):
    n_exp, d, f2 = w1.shape
    tk = _pick(512, d)
    perm = np.zeros((MXU_DIM, MXU_DIM), np.float32)
    perm[np.arange(0, MXU_DIM, 2), np.arange(LANES)] = 1.0
    perm[np.arange(1, MXU_DIM, 2), LANES + np.arange(LANES)] = 1.0
    out = jax.ShapeDtypeStruct((n_exp, d, f2 // 2), BF16)
    return pl.pallas_call(
        _deinterleave_kernel,
        out_shape=(out, out),
        grid=(n_exp, d // tk),
        in_specs=[pl.BlockSpec((1, tk, f2), lambda e, k: (e, k, 0)),
                  pl.BlockSpec((MXU_DIM, MXU_DIM), lambda e, k: (0, 0))],
        out_specs=(pl.BlockSpec((1, tk, f2 // 2), lambda e, k: (e, k, 0)),
                   pl.BlockSpec((1, tk, f2 // 2), lambda e, k: (e, k, 0))),
        compiler_params=_params(("parallel", "parallel")),
        name="moe_deinterleave",
    )(w1, jnp.asarray(perm, BF16))


def _moe_kernel(blk_e_ref, nact_ref, nval_ref, tok_ref, tok_next_ref, tgt_ref, m_hbm, w1g_ref, w1l_ref,
                b1g_ref, b1l_ref, w2_ref, b2_ref, y_hbm, xbuf, acc, sem_in, sem_out, *, blk, n_f, n_blocks):
    i = pl.program_id(0)
    f = pl.program_id(1)
    n_act = nact_ref[0]
    active = i < n_act
    slot = i % 2

    def row_in(idx_ref, r, s):
        return pltpu.make_async_copy(m_hbm.at[pl.ds(idx_ref[0, 0, r], 1)], xbuf.at[s, pl.ds(r, 1)], sem_in.at[s])

    def row_out(r):
        return pltpu.make_async_copy(acc.at[pl.ds(r, 1)], y_hbm.at[pl.ds(tgt_ref[0, 0, r], 1)], sem_out)

    def wait_rows(s):
        lax.fori_loop(0, blk, lambda r, c: (row_in(tok_ref, r, s).wait(), c)[1], 0, unroll=8)

    @pl.when(jnp.logical_and(i == 0, f == 0))
    def _first_gather():
        lax.fori_loop(0, blk, lambda r, c: (row_in(tok_ref, r, 0).start(), c)[1], 0, unroll=8)

    @pl.when(jnp.logical_and(i <= n_act, f == 0))
    def _wait_gather():
        wait_rows(slot)

    def compute(first):
        if first:
            for r in range(blk):
                row_in(tok_next_ref, r, 1 - slot).start()
        x = xbuf[slot].astype(BF16)
        hg = jnp.dot(x, w1g_ref[0], preferred_element_type=F32) + b1g_ref[0]
        hl = jnp.dot(x, w1l_ref[0], preferred_element_type=F32) + b1l_ref[0]
        x_glu = jnp.minimum(hg, SWIGLU_LIMIT)
        x_lin = jnp.clip(hl, -SWIGLU_LIMIT, SWIGLU_LIMIT)
        act = x_glu * jax.nn.sigmoid(SWIGLU_ALPHA * x_glu) * (x_lin + 1.0)
        contrib = jnp.dot(act.astype(BF16), w2_ref[0], preferred_element_type=F32)
        if first:
            acc[...] = contrib + b2_ref[0]
        else:
            acc[...] += contrib

    @pl.when(jnp.logical_and(active, f == 0))
    def _compute_first():
        compute(True)

    @pl.when(jnp.logical_and(active, f > 0))
    def _compute_rest():
        compute(False)

    @pl.when(jnp.logical_and(jnp.logical_and(active, i == n_blocks - 1), f == n_f - 1))
    def _drain():
        wait_rows(1 - slot)

    @pl.when(jnp.logical_and(active, f == n_f - 1))
    def _scatter():
        n_valid = nval_ref[i]
        lax.fori_loop(0, n_valid, lambda r, c: (row_out(r).start(), c)[1], 0)
        lax.fori_loop(0, n_valid, lambda r, c: (row_out(r).wait(), c)[1], 0)


def _moe_branch(h1, g_moe, w_router, b_router, w1, b1, w2, b2):
    n_tok, d = h1.shape
    n_exp, _, f2 = w1.shape
    fdim = f2 // 2
    bm = _pick(256, n_tok)
    wr = jnp.zeros((d, LANES), F32).at[:, :n_exp].set(w_router)
    br = jnp.zeros((1, LANES), F32).at[0, :n_exp].set(b_router)
    m, route = pl.pallas_call(
        functools.partial(_router_kernel, n_experts=n_exp),
        out_shape=(jax.ShapeDtypeStruct((n_tok, d), F32), jax.ShapeDtypeStruct((n_tok, LANES), F32)),
        grid=(n_tok // bm,),
        in_specs=[pl.BlockSpec((bm, d), lambda i: (i, 0)), pl.BlockSpec((1, d), lambda i: (0, 0)),
                  pl.BlockSpec((d, LANES), lambda i: (0, 0)), pl.BlockSpec((1, LANES), lambda i: (0, 0))],
        out_specs=(pl.BlockSpec((bm, d), lambda i: (i, 0)), pl.BlockSpec((bm, LANES), lambda i: (i, 0))),
        compiler_params=_params(("parallel",)),
        name="moe_router",
    )(h1, g_moe.reshape(1, d), wr, br)
    top_e = route[:, :TOP_K].astype(jnp.int32)

    n_assign = n_tok * TOP_K
    blk = _pick(512, n_assign)
    n_blocks = n_assign // blk + n_exp
    n_rows = n_blocks * blk
    e_flat = top_e.reshape(-1)
    onehot = (e_flat[:, None] == jnp.arange(n_exp, dtype=jnp.int32)[None, :]).astype(jnp.int32)
    csum = jnp.cumsum(onehot, axis=0)
    rank = jnp.take_along_axis(csum, e_flat[:, None], axis=1)[:, 0] - 1
    counts = csum[-1]
    padded = (counts + blk - 1) // blk * blk
    pend = jnp.cumsum(padded)
    pstart = pend - padded
    dest = pstart[e_flat] + rank
    row_a = jnp.zeros((n_rows,), jnp.int32).at[dest].set(jnp.arange(1, n_assign + 1, dtype=jnp.int32)) - 1
    row_tok = jnp.maximum(row_a, 0) // TOP_K
    row_tgt = (jnp.maximum(row_a, 0) % TOP_K) * n_tok + row_tok
    n_act = (pend[-1] // blk).astype(jnp.int32)
    blk_ids = jnp.minimum(jnp.arange(n_blocks, dtype=jnp.int32), n_act - 1)
    blk_e = jnp.minimum(jnp.searchsorted(pend, blk_ids * blk, side='right'), n_exp - 1).astype(jnp.int32)
    n_valid = jnp.clip(pstart[blk_e] + counts[blk_e] - blk_ids * blk, 0, blk).astype(jnp.int32)

    fc = _pick(512, fdim)
    n_f = fdim // fc
    w1g, w1l = _deinterleave(w1)
    b1g = b1[:, 0::2].reshape(n_exp, 1, fdim)
    b1l = b1[:, 1::2].reshape(n_exp, 1, fdim)
    grid_spec = pltpu.PrefetchScalarGridSpec(
        num_scalar_prefetch=3,
        grid=(n_blocks, n_f),
        in_specs=[
            pl.BlockSpec((1, 1, blk), lambda i, f, be, na, nv: (i, 0, 0), memory_space=pltpu.SMEM),
            pl.BlockSpec((1, 1, blk), lambda i, f, be, na, nv: (jnp.minimum(i + 1, n_blocks - 1), 0, 0),
                         memory_space=pltpu.SMEM),
            pl.BlockSpec((1, 1, blk), lambda i, f, be, na, nv: (i, 0, 0), memory_space=pltpu.SMEM),
            pl.BlockSpec(memory_space=pl.ANY),
            pl.BlockSpec((1, d, fc), lambda i, f, be, na, nv: (be[i], 0, f)),
            pl.BlockSpec((1, d, fc), lambda i, f, be, na, nv: (be[i], 0, f)),
            pl.BlockSpec((1, 1, fc), lambda i, f, be, na, nv: (be[i], 0, f)),
            pl.BlockSpec((1, 1, fc), lambda i, f, be, na, nv: (be[i], 0, f)),
            pl.BlockSpec((1, fc, d), lambda i, f, be, na, nv: (be[i], f, 0)),
            pl.BlockSpec((1, 1, d), lambda i, f, be, na, nv: (be[i], 0, 0)),
        ],
        out_specs=pl.BlockSpec(memory_space=pl.ANY),
        scratch_shapes=[pltpu.VMEM((2, blk, d), F32), pltpu.VMEM((blk, d), F32),
                        pltpu.SemaphoreType.DMA((2,)), pltpu.SemaphoreType.DMA],
    )
    y4 = pl.pallas_call(
        functools.partial(_moe_kernel, blk=blk, n_f=n_f, n_blocks=n_blocks),
        out_shape=jax.ShapeDtypeStruct((n_assign, d), F32),
        grid_spec=grid_spec,
        compiler_params=pltpu.CompilerParams(dimension_semantics=("arbitrary", "arbitrary"),
                                             vmem_limit_bytes=VMEM_LIMIT, has_side_effects=True),
        name="moe_experts",
    )(blk_e, n_act.reshape(1), n_valid, row_tok.reshape(n_blocks, 1, blk), row_tok.reshape(n_blocks, 1, blk),
      row_tgt.reshape(n_blocks, 1, blk),
      m, w1g, w1l, b1g, b1l, w2.astype(BF16), b2.reshape(n_exp, 1, d))
    return y4, route


def _combine_kernel(h_ref, y0_ref, y1_ref, y2_ref, y3_ref, r_ref, g_ref, h2_ref, n2_ref):
    r = r_ref[...]
    gk = [r[:, TOP_K + k:TOP_K + k + 1] for k in range(TOP_K)]
    h2 = h_ref[...] + ((gk[0] * y0_ref[...] + gk[1] * y1_ref[...]) + (gk[2] * y2_ref[...] + gk[3] * y3_ref[...]))
    h2_ref[...] = h2
    ms = jnp.mean(h2 * h2, axis=-1, keepdims=True)
    n2_ref[...] = (h2 * lax.rsqrt(ms + EPS) * g_ref[...]).astype(n2_ref.dtype)


def _combine(h1, y4, route, g_ple):
    n_tok, d = h1.shape
    bm = _pick(128, n_tok)
    nb = n_tok // bm
    yspec = [pl.BlockSpec((bm, d), functools.partial(lambda i, k: (k * nb + i, 0), k=k)) for k in range(TOP_K)]
    return pl.pallas_call(
        _combine_kernel,
        out_shape=(jax.ShapeDtypeStruct((n_tok, d), F32), jax.ShapeDtypeStruct((n_tok, d), BF16)),
        grid=(nb,),
        in_specs=[pl.BlockSpec((bm, d), lambda i: (i, 0))] + yspec
        + [pl.BlockSpec((bm, LANES), lambda i: (i, 0)), pl.BlockSpec((1, d), lambda i: (0, 0))],
        out_specs=(pl.BlockSpec((bm, d), lambda i: (i, 0)), pl.BlockSpec((bm, d), lambda i: (i, 0))),
        compiler_params=_params(("parallel",)),
        name="moe_combine",
    )(h1, y4, y4, y4, y4, route, g_ple.reshape(1, d))


def _rope_tables(seq, rot_dim, width):
    half = rot_dim // 2
    inv = ROPE_THETA ** (-jnp.arange(half, dtype=F32) / half)
    ang = jnp.arange(seq, dtype=F32)[:, None] * inv[None, :]
    return jnp.cos(ang), jnp.sin(ang), half


def _layer(h, p_i, g_mix, w_in, b_gate, lam_re, lam_im, log_dt, b_re, b_im, c_re, c_im, d_skip, w_glu,
           w_ssm_out, w_attn_out, w_o, g_moe, w_router, b_router, w1, b1, w2, b2, g_ple, w_ple,
           w_ple_gate, b_ple_gate, bsz, seq):
    n_tok, d = h.shape
    ssm_w = w_glu.shape[0]
    aw = w_attn_out.shape[0]
    n_heads = aw // HEAD_DIM
    n_idx = (w_in.shape[1] - ssm_w - 3 * aw - IDX_DIM - 2 * d) // (IDX_DIM + 1)
    o_u, o_qkv, o_qi, o_ki, o_wi, o_gt = np.cumsum([0, ssm_w, 3 * aw, n_idx * IDX_DIM, IDX_DIM, n_idx]).tolist()
    bm_t = _pick(1024, seq)
    nbt = seq // bm_t

    a = _rmsnorm(h, g_mix, BF16, "rms_mix")
    w_a16 = w_in[:, :o_ki + 2 * IDX_DIM].astype(BF16)
    w_g16 = w_in[:, o_gt:].astype(BF16)

    u = _matmul([(a, w_a16, o_u)], [], _epi_plain, ssm_w, F32, 1024, 1024, "proj_u")

    cos, sin, _ = _rope_tables(seq, HEAD_DIM, HEAD_DIM)
    cos_f = jnp.concatenate([cos, cos], axis=1)
    sin_f = jnp.concatenate([-sin, sin], axis=1)
    tab = lambda t: (t, (bm_t, LANES), lambda i, j: (i % nbt, 0))
    qk = _matmul([(a, w_a16, o_qkv)], [tab(cos_f), tab(sin_f)], _epi_rope_full,
                 2 * aw, BF16, bm_t, 1024, "proj_qk")
    v = _matmul([(a, w_a16, o_qkv + 2 * aw)], [], _epi_plain, aw, BF16, 1024, 1024, "proj_v")

    cos_i, sin_i, half = _rope_tables(seq, IDX_ROPE_DIM, IDX_DIM)
    zeros = jnp.zeros((seq, half), F32)
    cos_i = jnp.concatenate([cos_i, cos_i, jnp.ones((seq, IDX_DIM - 2 * half), F32)], axis=1)
    sin_a = jnp.concatenate([-sin_i, zeros, jnp.zeros((seq, IDX_DIM - 2 * half), F32)], axis=1)
    sin_b = jnp.concatenate([zeros, sin_i, jnp.zeros((seq, IDX_DIM - 2 * half), F32)], axis=1)
    itabs = [tab(cos_i), tab(sin_a), tab(sin_b)]
    qi = _matmul([(a, w_a16, o_qi)], itabs, _epi_rope_idx, n_idx * IDX_DIM, BF16, bm_t, 1024, "proj_qi",
                 head_major=True)
    kiwi = _matmul([(a, w_a16, o_ki)], itabs, _epi_kiwi, 2 * IDX_DIM, F32, bm_t, 2 * IDX_DIM, "proj_kiwi")

    gates = _matmul([(a, w_g16)],
                    [(b_gate.reshape(1, 2 * d), (1, _pick(1024, 2 * d)), lambda i, j: (0, j))],
                    _epi_sigmoid_bias, 2 * d, BF16, 1024, 1024, "proj_gates")

    g_ssm = _s5_branch(u, bsz, seq, lam_re, lam_im, log_dt, b_re, b_im, c_re, c_im, d_skip)
    bn_g = _pick(1024, ssm_w)
    y_ssm = _matmul([(g_ssm, w_glu.astype(BF16))], [(g_ssm, (_pick(1024, n_tok), bn_g), lambda i, j: (i, j))],
                    _epi_glu, ssm_w, BF16, 1024, 1024, "ssm_glu")

    y_att = _dsa_branch(qk, v, qi, kiwi, bsz, seq, n_heads, n_idx)

    bm_m, bn_m = _pick(512, n_tok), _pick(1024, d)
    nj = d // bn_m
    merged = _matmul([(y_ssm, w_ssm_out.astype(BF16)), (y_att, w_attn_out.astype(BF16))],
                     [(gates, (bm_m, bn_m), lambda i, j: (i, j)), (gates, (bm_m, bn_m), lambda i, j: (i, nj + j))],
                     _epi_merge, d, BF16, bm_m, bn_m, "merge")
    h1 = _matmul([(merged, w_o.astype(BF16))], [(h, (bm_m, bn_m), lambda i, j: (i, j))],
                 _epi_residual, d, F32, bm_m, bn_m, "out_proj")

    y4, route = _moe_branch(h1, g_moe, w_router, b_router, w1, b1, w2, b2)
    h2, n2 = _combine(h1, y4, route, g_ple)

    h3 = _matmul([(n2, w_ple_gate.astype(BF16)), (p_i.astype(BF16), w_ple.astype(BF16))],
                 [(h2, (bm_m, bn_m), lambda i, j: (i, j)), (b_ple_gate.reshape(1, d), (1, bn_m), lambda i, j: (0, j))],
                 _epi_ple, d, F32, bm_m, bn_m, "ple")
    return h3


def kernel(x, p, g_mix, w_in, b_gate, ssm_lambda_re, ssm_lambda_im, ssm_log_dt, ssm_b_re, ssm_b_im, ssm_c_re, ssm_c_im, ssm_d, ssm_w_glu, w_ssm_out, w_attn_out, w_o, g_moe, w_router, b_router, w_expert_in, b_expert_in, w_expert_out, b_expert_out, g_ple, w_ple, w_ple_gate, b_ple_gate, g_final):
    bsz, seq, d = x.shape
    h = x.reshape(bsz * seq, d)
    for i in range(p.shape[0]):
        h = _layer(h, p[i].reshape(bsz * seq, -1), g_mix[i], w_in[i], b_gate[i], ssm_lambda_re[i],
                   ssm_lambda_im[i], ssm_log_dt[i], ssm_b_re[i], ssm_b_im[i], ssm_c_re[i], ssm_c_im[i],
                   ssm_d[i], ssm_w_glu[i], w_ssm_out[i], w_attn_out[i], w_o[i], g_moe[i], w_router[i],
                   b_router[i], w_expert_in[i], b_expert_in[i], w_expert_out[i], b_expert_out[i],
                   g_ple[i], w_ple[i], w_ple_gate[i], b_ple_gate[i], bsz, seq)
    return _rmsnorm(h, g_final, F32, "rms_final").reshape(bsz, seq, d)
```

```python
import functools
import math

import jax
import jax.numpy as jnp
import numpy as np
from jax import lax
from jax.experimental import pallas as pl
from jax.experimental.pallas import tpu as pltpu

EPS = 1e-6
HEAD_DIM = 128
IDX_DIM = 128
IDX_ROPE_DIM = 64
TOPK_MAX = 256
ROPE_THETA = 10000.0
TOP_K = 4
SWIGLU_ALPHA = 1.702
SWIGLU_LIMIT = 7.0
LANES = 128
MXU_DIM = 256
VMEM_LIMIT = 56 * 1024 * 1024
NEG_BIG = -1e30
ATTN_HEADS_PER_STEP = 2
ATTN_KEY_CHUNK = 1024
ATTN_Q_BLOCK = 512
INDEX_Q_BLOCK = 256
S5_BATCH_BLOCK = 2

F32 = jnp.float32
BF16 = jnp.bfloat16


def _pick(pref, n):
    b = min(pref, n)
    while n % b:
        b -= 1
    return b


def _params(sem):
    return pltpu.CompilerParams(dimension_semantics=sem, vmem_limit_bytes=VMEM_LIMIT)


def _rms_kernel(x_ref, g_ref, o_ref):
    x = x_ref[...]
    ms = jnp.mean(x * x, axis=-1, keepdims=True)
    o_ref[...] = (x * lax.rsqrt(ms + EPS) * g_ref[...]).astype(o_ref.dtype)


def _rmsnorm(x2d, g, out_dtype, name):
    m, d = x2d.shape
    bm = _pick(256, m)
    return pl.pallas_call(
        _rms_kernel,
        out_shape=jax.ShapeDtypeStruct((m, d), out_dtype),
        grid=(m // bm,),
        in_specs=[pl.BlockSpec((bm, d), lambda i: (i, 0)), pl.BlockSpec((1, d), lambda i: (0, 0))],
        out_specs=pl.BlockSpec((bm, d), lambda i: (i, 0)),
        compiler_params=_params(("parallel",)),
        name=name,
    )(x2d, g.reshape(1, d))


def _mm_kernel(*refs, n_lhs, n_extra, epilogue):
    lhs = refs[: 2 * n_lhs]
    extras = refs[2 * n_lhs: 2 * n_lhs + n_extra]
    o_ref = refs[2 * n_lhs + n_extra]
    accs = [jnp.dot(lhs[2 * t][...], lhs[2 * t + 1][...], preferred_element_type=F32) for t in range(n_lhs)]
    epilogue(accs, extras, o_ref)


def _matmul(pairs, extras, epilogue, n_out, out_dtype, bm, bn, name, head_major=False):
    m = pairs[0][0].shape[0]
    bm = _pick(bm, m)
    bn = _pick(bn, n_out)
    in_specs, args = [], []
    for pair in pairs:
        a, w = pair[:2]
        off = pair[2] if len(pair) > 2 else 0
        if off % bn:
            w, off = w[:, off:off + n_out], 0
        k = a.shape[1]
        in_specs += [pl.BlockSpec((bm, k), lambda i, j: (i, 0)),
                     pl.BlockSpec((k, bn), functools.partial(lambda i, j, ob: (0, j + ob), ob=off // bn))]
        args += [a, w]
    for arr, bshape, imap in extras:
        in_specs.append(pl.BlockSpec(bshape, imap))
        args.append(arr)
    if head_major:
        out_shape = jax.ShapeDtypeStruct((n_out // LANES, m, LANES), out_dtype)
        out_spec = pl.BlockSpec((bn // LANES, bm, LANES), lambda i, j: (j, i, 0))
    else:
        out_shape = jax.ShapeDtypeStruct((m, n_out), out_dtype)
        out_spec = pl.BlockSpec((bm, bn), lambda i, j: (i, j))
    return pl.pallas_call(
        functools.partial(_mm_kernel, n_lhs=len(pairs), n_extra=len(extras), epilogue=epilogue),
        out_shape=out_shape,
        grid=(m // bm, n_out // bn),
        in_specs=in_specs,
        out_specs=out_spec,
        compiler_params=_params(("parallel", "arbitrary")),
        name=name,
    )(*args)


def _epi_plain(accs, extras, o_ref):
    o_ref[...] = accs[0].astype(o_ref.dtype)


def _epi_rope_full(accs, extras, o_ref):
    cos, sin = extras[0][...], extras[1][...]
    acc = accs[0]
    for h in range(acc.shape[1] // HEAD_DIM):
        xh = acc[:, h * HEAD_DIM:(h + 1) * HEAD_DIM]
        o_ref[:, h * HEAD_DIM:(h + 1) * HEAD_DIM] = (
            xh * cos + pltpu.roll(xh, HEAD_DIM // 2, 1) * sin).astype(o_ref.dtype)


def _rope_idx(xh, cos, sin_a, sin_b):
    half = IDX_ROPE_DIM // 2
    return xh * cos + pltpu.roll(xh, IDX_DIM - half, 1) * sin_a + pltpu.roll(xh, half, 1) * sin_b


def _epi_rope_idx(accs, extras, o_ref):
    cos, sin_a, sin_b = extras[0][...], extras[1][...], extras[2][...]
    acc = accs[0]
    for h in range(acc.shape[1] // IDX_DIM):
        xh = acc[:, h * IDX_DIM:(h + 1) * IDX_DIM]
        o_ref[h] = _rope_idx(xh, cos, sin_a, sin_b).astype(o_ref.dtype)


def _epi_kiwi(accs, extras, o_ref):
    cos, sin_a, sin_b = extras[0][...], extras[1][...], extras[2][...]
    acc = accs[0]
    o_ref[:, :IDX_DIM] = _rope_idx(acc[:, :IDX_DIM], cos, sin_a, sin_b).astype(o_ref.dtype)
    o_ref[:, IDX_DIM:] = acc[:, IDX_DIM:].astype(o_ref.dtype)


def _epi_sigmoid_bias(accs, extras, o_ref):
    o_ref[...] = jax.nn.sigmoid(accs[0] + extras[0][...]).astype(o_ref.dtype)


def _epi_glu(accs, extras, o_ref):
    g = extras[0][...].astype(F32)
    o_ref[...] = (g * jax.nn.sigmoid(accs[0])).astype(o_ref.dtype)


def _epi_merge(accs, extras, o_ref):
    g0 = extras[0][...].astype(F32)
    g1 = extras[1][...].astype(F32)
    o_ref[...] = (g0 * accs[0] + g1 * accs[1]).astype(o_ref.dtype)


def _epi_residual(accs, extras, o_ref):
    o_ref[...] = (extras[0][...] + accs[0]).astype(o_ref.dtype)


def _epi_ple(accs, extras, o_ref):
    h, b = extras[0][...], extras[1][...]
    o_ref[...] = (h + accs[1] * jax.nn.sigmoid(accs[0] + b)).astype(o_ref.dtype)


def _s5_kernel(u_ref, w1_ref, w2_ref, w3_ref, ar_ref, ai_ref, d_ref, o_ref, y_scr, w1b, w2b, w3b,
               *, n_chunks, chunk):
    gpb, lp, _ = w1_ref.shape
    p = lp // chunk
    n = w2_ref.shape[2] // 2
    per_half = LANES // p

    @pl.when(pl.program_id(1) == 0)
    def _expand_weights():
        lane = lax.broadcasted_iota(jnp.int32, (1, LANES), 1)
        w2b[...] = jnp.zeros(w2b.shape, BF16)

        def group(gl, carry):
            w1g, w2g, w3g = w1_ref[gl], w2_ref[gl], w3_ref[gl]
            row0 = gl * p
            in_group = lane // p == gl
            for t in range(chunk):
                hf, t8 = divmod(t, per_half)
                shift = (gl * p - t8 * p) % LANES
                src = slice(hf * LANES, (hf + 1) * LANES)
                dst = slice(t * LANES, (t + 1) * LANES)
                p1 = jnp.where(in_group, pltpu.roll(w1g[:, src], shift, 1), 0.0).astype(BF16)
                for s in range(chunk):
                    w1b[pl.ds(pl.multiple_of(s * LANES + row0, p), p), dst] = p1[s * p:(s + 1) * p, :]
                p3 = jnp.where(in_group, pltpu.roll(w3g[:, src], shift, 1), 0.0).astype(BF16)
                for r in range(2):
                    w3b[pl.ds(pl.multiple_of(r * gpb * n + gl * n, n), n), dst] = p3[r * n:(r + 1) * n, :]
            for r in range(2):
                col = r * gpb * n + gl * n
                cb = pl.multiple_of(col // LANES * LANES, LANES)
                p2 = jnp.where(lane // n == (col % LANES) // n,
                               pltpu.roll(w2g, (col - r * n) % LANES, 1), 0.0).astype(BF16)
                for s in range(chunk):
                    w2b[pl.ds(pl.multiple_of(s * LANES + row0, p), p), pl.ds(cb, LANES)] = p2[s * p:(s + 1) * p, :]
            return carry

        lax.fori_loop(0, gpb, group, 0)

    rows = u_ref.shape[0] // chunk
    us = [u_ref[pl.ds(s, rows, stride=chunk), :] for s in range(chunk)]
    u = jnp.concatenate(us, axis=1).astype(BF16)
    x = jnp.dot(u, w2b[...], preferred_element_type=F32)
    cidx = lax.broadcasted_iota(jnp.int32, (rows, 1), 0) % n_chunks
    half = x.shape[1] // 2
    for k in range(int(math.log2(n_chunks))):
        sh = 1 << k
        s = jnp.where(cidx >= sh, pltpu.roll(x, sh, 0), 0.0)
        x = x + ar_ref[0, k:k + 1, :] * s + ai_ref[0, k:k + 1, :] * pltpu.roll(s, half, 1)
    xprev = jnp.where(cidx >= 1, pltpu.roll(x, 1, 0), 0.0)
    y = (jnp.dot(u, w1b[...], preferred_element_type=F32)
         + jnp.dot(xprev.astype(BF16), w3b[...], preferred_element_type=F32))
    d = d_ref[...]
    for s in range(chunk):
        y_scr[pl.ds(s, rows, stride=chunk), :] = jax.nn.gelu(y[:, s * LANES:(s + 1) * LANES] + d * us[s])
    o_ref[...] = y_scr[...].astype(o_ref.dtype)


def _s5_weights(lam_re, lam_im, log_dt, b_re, b_im, c_re, c_im, chunk, n_chunks):
    hi = lax.Precision.HIGHEST
    g, n, p = b_re.shape
    dt = jnp.exp(log_dt)[:, None]
    mag = jnp.exp(lam_re * dt)
    lb_re, lb_im = mag * jnp.cos(lam_im * dt), mag * jnp.sin(lam_im * dt)
    den = lam_re * lam_re + lam_im * lam_im
    nr, ni = lb_re - 1.0, lb_im
    f_re = (nr * lam_re + ni * lam_im) / den
    f_im = (ni * lam_re - nr * lam_im) / den
    bb_re = f_re[..., None] * b_re - f_im[..., None] * b_im
    bb_im = f_re[..., None] * b_im + f_im[..., None] * b_re

    def power(d):
        m = jnp.exp(lam_re * dt * d[:, None, None])
        ang = lam_im * dt * d[:, None, None]
        return m * jnp.cos(ang), m * jnp.sin(ang)

    pw_re, pw_im = power(jnp.arange(chunk + 1, dtype=F32))
    cp_re = c_re[None] * pw_re[:, :, None, :] - c_im[None] * pw_im[:, :, None, :]
    cp_im = c_re[None] * pw_im[:, :, None, :] + c_im[None] * pw_re[:, :, None, :]
    kd = (jnp.einsum('dgpn,gnj->dgpj', cp_re[:chunk], bb_re, precision=hi)
          - jnp.einsum('dgpn,gnj->dgpj', cp_im[:chunk], bb_im, precision=hi))
    s_idx = np.arange(chunk)[:, None]
    t_idx = np.arange(chunk)[None, :]
    lag = np.clip(t_idx - s_idx, 0, chunk - 1)
    w1 = jnp.where((t_idx >= s_idx)[:, :, None, None, None], kd[lag], 0.0)
    w1 = w1.transpose(2, 0, 4, 1, 3).reshape(g, chunk * p, chunk * p)
    rev_re, rev_im = pw_re[:chunk][::-1], pw_im[:chunk][::-1]
    z_re = rev_re[..., None] * bb_re[None] - rev_im[..., None] * bb_im[None]
    z_im = rev_re[..., None] * bb_im[None] + rev_im[..., None] * bb_re[None]
    w2 = jnp.concatenate([z_re, z_im], axis=2).transpose(1, 0, 3, 2).reshape(g, chunk * p, 2 * n)
    w3 = jnp.concatenate([cp_re[1:], -cp_im[1:]], axis=3)
    w3 = w3.transpose(1, 3, 0, 2).reshape(g, 2 * n, chunk * p)
    n_steps = int(math.log2(n_chunks))
    a_re, a_im = power(chunk * (2.0 ** jnp.arange(n_steps, dtype=F32)))
    ar = jnp.concatenate([a_re, a_re], axis=2).transpose(1, 0, 2)
    ai = jnp.concatenate([-a_im, a_im], axis=2).transpose(1, 0, 2)
    gpb = LANES // p
    sg = g // gpb
    arb = ar.reshape(sg, gpb, n_steps, 2, n).transpose(0, 2, 3, 1, 4).reshape(sg, n_steps, 2 * gpb * n)
    aib = ai.reshape(sg, gpb, n_steps, 2, n).transpose(0, 2, 3, 1, 4).reshape(sg, n_steps, 2 * gpb * n)
    return w1, w2, w3, arb, aib


def _s5_branch(u2d, bsz, seq, lam_re, lam_im, log_dt, b_re, b_im, c_re, c_im, d_skip):
    g, n, p = b_re.shape
    chunk = MXU_DIM // p
    n_chunks = seq // chunk
    assert n_chunks & (n_chunks - 1) == 0 and 2 * n == LANES
    w1, w2, w3, ar, ai = _s5_weights(lam_re, lam_im, log_dt, b_re, b_im, c_re, c_im, chunk, n_chunks)
    n_steps = ar.shape[1]
    gpb = LANES // p
    assert g % gpb == 0
    n_tok = bsz * seq
    bb = _pick(S5_BATCH_BLOCK, bsz)
    trows = bb * seq
    lp = chunk * p
    lw, sw = chunk * LANES, 2 * gpb * n
    return pl.pallas_call(
        functools.partial(_s5_kernel, n_chunks=n_chunks, chunk=chunk),
        out_shape=jax.ShapeDtypeStruct((n_tok, g * p), BF16),
        grid=(g // gpb, bsz // bb),
        in_specs=[
            pl.BlockSpec((trows, LANES), lambda i, b: (b, i)),
            pl.BlockSpec((gpb, lp, lp), lambda i, b: (i, 0, 0)),
            pl.BlockSpec((gpb, lp, 2 * n), lambda i, b: (i, 0, 0)),
            pl.BlockSpec((gpb, 2 * n, lp), lambda i, b: (i, 0, 0)),
            pl.BlockSpec((1, n_steps, sw), lambda i, b: (i, 0, 0)),
            pl.BlockSpec((1, n_steps, sw), lambda i, b: (i, 0, 0)),
            pl.BlockSpec((1, LANES), lambda i, b: (0, i)),
        ],
        out_specs=pl.BlockSpec((trows, LANES), lambda i, b: (b, i)),
        scratch_shapes=[pltpu.VMEM((trows, LANES), F32), pltpu.VMEM((lw, lw), BF16), pltpu.VMEM((lw, sw), BF16),
                        pltpu.VMEM((sw, lw), BF16)],
        compiler_params=_params(("parallel", "arbitrary")),
        name="s5_scan",
    )(u2d, w1, w2, w3, ar, ai, d_skip.reshape(1, g * p))


def _dsa_index_kernel(qi_ref, ki_ref, wi_ref, o_ref, key_ref, *, n_heads, topk, scale, kc):
    qb, seq = o_ref.shape
    t0 = pl.program_id(1) * qb
    n_kc = (t0 + qb + kc - 1) // kc
    w = wi_ref[...] * scale
    wcols = [w[:, h:h + 1] for h in range(n_heads)]
    row = t0 + lax.broadcasted_iota(jnp.int32, (qb, 1), 0)

    def chunk_causal(c):
        return c * kc + lax.broadcasted_iota(jnp.int32, (1, kc), 1) <= row

    def scores(c, carry):
        off = pl.multiple_of(c * kc, kc)
        kchunk = ki_ref[pl.ds(off, kc), :]
        s = jnp.zeros((qb, kc), F32)
        for h in range(n_heads):
            z = lax.dot_general(qi_ref[h], kchunk, (((1,), (1,)), ((), ())), preferred_element_type=F32)
            s = s + wcols[h] * jnp.maximum(z, 0.0)
        s = jnp.where(chunk_causal(c), s, -jnp.inf)
        bits = lax.bitcast_convert_type(s, jnp.int32)
        key_ref[:, pl.ds(off, kc)] = jnp.where(bits < 0, bits ^ jnp.int32(0x7FFFFFFF), bits)
        return carry

    lax.fori_loop(0, n_kc, scores, 0)

    def bisect(it, thr):
        cand = thr + lax.shift_left(jnp.int32(1), jnp.int32(31) - jnp.asarray(it, jnp.int32))

        def count(c, acc):
            k = key_ref[:, pl.ds(pl.multiple_of(c * kc, kc), kc)]
            hit = jnp.where(k >= cand, 1.0, 0.0)
            for j in range(kc // LANES):
                acc = acc + hit[:, j * LANES:(j + 1) * LANES]
            return acc

        acc = lax.fori_loop(0, n_kc, count, jnp.zeros((qb, LANES), F32))
        cnt = jnp.sum(acc, axis=1, keepdims=True)
        return jnp.where(cnt >= topk, cand, thr)

    thr = lax.fori_loop(0, 32, bisect, jnp.full((qb, 1), jnp.iinfo(jnp.int32).min, jnp.int32))

    o_ref[...] = jnp.full(o_ref.shape, NEG_BIG, o_ref.dtype)

    def emit(c, carry):
        off = pl.multiple_of(c * kc, kc)
        k = key_ref[:, pl.ds(off, kc)]
        bias = jnp.where(k >= thr, jnp.where(chunk_causal(c), 0.0, NEG_BIG), NEG_BIG)
        o_ref[:, pl.ds(off, kc)] = bias.astype(o_ref.dtype)
        return carry

    lax.fori_loop(0, n_kc, emit, 0)


def _dsa_attn_kernel(q_ref, k_ref, v_ref, m_ref, o_ref, *, scale, kc):
    qa = q_ref.shape[0]
    n_h = q_ref.shape[1] // HEAD_DIM
    n_kc = ((pl.program_id(2) + 1) * qa + kc - 1) // kc
    qs = [(q_ref[:, h * HEAD_DIM:(h + 1) * HEAD_DIM].astype(F32) * (scale * math.log2(math.e))).astype(BF16)
          for h in range(n_h)]

    def chunk(c, carry):
        off = pl.multiple_of(c * kc, kc)
        bias = m_ref[:, pl.ds(off, kc)].astype(F32)
        out = []
        for h in range(n_h):
            m, l, acc = carry[h]
            hs = slice(h * HEAD_DIM, (h + 1) * HEAD_DIM)
            s = lax.dot_general(qs[h], k_ref[pl.ds(off, kc), hs], (((1,), (1,)), ((), ())),
                                preferred_element_type=F32) + bias
            m_new = jnp.maximum(m, jnp.max(s, axis=1, keepdims=True))
            alpha = jnp.exp2(m - m_new)
            p = jnp.exp2(s - m_new)
            l = alpha * l + jnp.sum(p, axis=1, keepdims=True)
            acc = alpha * acc + jnp.dot(p.astype(BF16), v_ref[pl.ds(off, kc), hs], preferred_element_type=F32)
            out.append((m_new, l, acc))
        return tuple(out)

    init = tuple((jnp.full((qa, 1), -jnp.inf, F32), jnp.zeros((qa, 1), F32), jnp.zeros((qa, HEAD_DIM), F32))
                 for _ in range(n_h))
    res = lax.fori_loop(0, n_kc, chunk, init)
    for h in range(n_h):
        _, l, acc = res[h]
        o_ref[:, h * HEAD_DIM:(h + 1) * HEAD_DIM] = (acc / l).astype(o_ref.dtype)


def _dsa_branch(qk, v, qi, kiwi, bsz, seq, n_heads, n_idx_heads):
    n_tok = bsz * seq
    aw = n_heads * HEAD_DIM
    topk = min(TOPK_MAX, seq // 4)
    qb = _pick(INDEX_Q_BLOCK, seq)
    nq = seq // qb
    mask = pl.pallas_call(
        functools.partial(_dsa_index_kernel, n_heads=n_idx_heads, topk=float(topk),
                          scale=(IDX_DIM ** -0.5) * (n_idx_heads ** -0.5), kc=_pick(MXU_DIM, seq)),
        out_shape=jax.ShapeDtypeStruct((n_tok, seq), BF16),
        grid=(bsz, nq),
        in_specs=[
            pl.BlockSpec((n_idx_heads, qb, IDX_DIM), lambda b, i: (0, b * nq + i, 0)),
            pl.BlockSpec((seq, IDX_DIM), lambda b, i: (b, 0)),
            pl.BlockSpec((qb, IDX_DIM), lambda b, i: (b * nq + i, 1)),
        ],
        out_specs=pl.BlockSpec((qb, seq), lambda b, i: (b * nq + i, 0)),
        scratch_shapes=[pltpu.VMEM((qb, seq), jnp.int32)],
        compiler_params=_params(("parallel", "arbitrary")),
        name="dsa_index",
    )(qi, kiwi.astype(BF16), kiwi)
    return _dsa_attend(qk, v, mask, bsz, seq, n_heads)


def _dsa_attend(qk, v, mask, bsz, seq, n_heads):
    n_tok = bsz * seq
    aw = n_heads * HEAD_DIM
    qa = _pick(ATTN_Q_BLOCK, seq)
    na = seq // qa
    hp = ATTN_HEADS_PER_STEP
    hw = hp * HEAD_DIM
    ng = n_heads // hp
    return pl.pallas_call(
        functools.partial(_dsa_attn_kernel, scale=HEAD_DIM ** -0.5, kc=_pick(ATTN_KEY_CHUNK, seq)),
        out_shape=jax.ShapeDtypeStruct((n_tok, aw), BF16),
        grid=(bsz, ng, na),
        in_specs=[
            pl.BlockSpec((qa, hw), lambda b, h, i: (b * na + i, h)),
            pl.BlockSpec((seq, hw), lambda b, h, i: (b, ng + h)),
            pl.BlockSpec((seq, hw), lambda b, h, i: (b, h)),
            pl.BlockSpec((qa, seq), lambda b, h, i: (b * na + i, 0)),
        ],
        out_specs=pl.BlockSpec((qa, hw), lambda b, h, i: (b * na + i, h)),
        compiler_params=_params(("parallel", "parallel", "arbitrary")),
        name="dsa_attn",
    )(qk, qk, v, mask)


def _router_kernel(h_ref, g_ref, wr_ref, br_ref, m_ref, r_ref, *, n_experts):
    x = h_ref[...]
    ms = jnp.mean(x * x, axis=-1, keepdims=True)
    m = x * lax.rsqrt(ms + EPS) * g_ref[...]
    m_ref[...] = m
    logits = jnp.dot(m, wr_ref[...], precision=lax.Precision.HIGHEST, preferred_element_type=F32) + br_ref[...]
    lane = lax.broadcasted_iota(jnp.int32, logits.shape, 1)
    l = jnp.where(lane < n_experts, logits, -jnp.inf)
    vals, idxs = [], []
    for _ in range(TOP_K):
        mx = jnp.max(l, axis=1, keepdims=True)
        am = jnp.min(jnp.where(l == mx, lane, LANES), axis=1, keepdims=True)
        vals.append(mx)
        idxs.append(am)
        l = jnp.where(lane == am, -jnp.inf, l)
    exps = [jnp.exp(v - vals[0]) for v in vals]
    tot = exps[0] + exps[1] + exps[2] + exps[3]
    out = jnp.zeros(logits.shape, F32)
    for k in range(TOP_K):
        out = jnp.where(lane == k, idxs[k].astype(F32), out)
        out = jnp.where(lane == TOP_K + k, exps[k] / tot, out)
    r_ref[...] = out


def _deinterleave_kernel(w_ref, p_ref, g_ref, l_ref):
    w = w_ref[0].astype(BF16)
    for c in range(w.shape[1] // MXU_DIM):
        z = jnp.dot(w[:, c * MXU_DIM:(c + 1) * MXU_DIM], p_ref[...], preferred_element_type=F32)
        g_ref[0, :, c * LANES:(c + 1) * LANES] = z[:, :LANES].astype(BF16)
        l_ref[0, :, c * LANES:(c + 1) * LANES] = z[:, LANES:].astype(BF16)


def _deinterleave(w1):
    n_exp, d, f2 = w1.shape
    tk = _pick(512, d)
    perm = np.zeros((MXU_DIM, MXU_DIM), np.float32)
    perm[np.arange(0, MXU_DIM, 2), np.arange(LANES)] = 1.0
    perm[np.arange(1, MXU_DIM, 2), LANES + np.arange(LANES)] = 1.0
    out = jax.ShapeDtypeStruct((n_exp, d, f2 // 2), BF16)
    return pl.pallas_call(
        _deinterleave_kernel,
        out_shape=(out, out),
        grid=(n_exp, d // tk),
        in_specs=[pl.BlockSpec((1, tk, f2), lambda e, k: (e, k, 0)),
                  pl.BlockSpec((MXU_DIM, MXU_DIM), lambda e, k: (0, 0))],
        out_specs=(pl.BlockSpec((1, tk, f2 // 2), lambda e, k: (e, k, 0)),
                   pl.BlockSpec((1, tk, f2 // 2), lambda e, k: (e, k, 0))),
        compiler_params=_params(("parallel", "parallel")),
        name="moe_deinterleave",
    )(w1, jnp.asarray(perm, BF16))


def _moe_kernel(blk_e_ref, nact_ref, nval_ref, tok_ref, tok_next_ref, tgt_ref, m_hbm, w1g_ref, w1l_ref,
                b1g_ref, b1l_ref, w2_ref, b2_ref, y_hbm, xbuf, acc, sem_in, sem_out, *, blk, n_f, n_blocks):
    i = pl.program_id(0)
    f = pl.program_id(1)
    n_act = nact_ref[0]
    active = i < n_act
    slot = i % 2

    def row_in(idx_ref, r, s):
        return pltpu.make_async_copy(m_hbm.at[pl.ds(idx_ref[0, 0, r], 1)], xbuf.at[s, pl.ds(r, 1)], sem_in.at[s])

    def row_out(r):
        return pltpu.make_async_copy(acc.at[pl.ds(r, 1)], y_hbm.at[pl.ds(tgt_ref[0, 0, r], 1)], sem_out)

    def wait_rows(s):
        lax.fori_loop(0, blk, lambda r, c: (row_in(tok_ref, r, s).wait(), c)[1], 0, unroll=8)

    @pl.when(jnp.logical_and(i == 0, f == 0))
    def _first_gather():
        lax.fori_loop(0, blk, lambda r, c: (row_in(tok_ref, r, 0).start(), c)[1], 0, unroll=8)

    @pl.when(jnp.logical_and(i <= n_act, f == 0))
    def _wait_gather():
        wait_rows(slot)

    def compute(first):
        if first:
            for r in range(blk):
                row_in(tok_next_ref, r, 1 - slot).start()
        x = xbuf[slot].astype(BF16)
        hg = jnp.dot(x, w1g_ref[0], preferred_element_type=F32) + b1g_ref[0]
        hl = jnp.dot(x, w1l_ref[0], preferred_element_type=F32) + b1l_ref[0]
        x_glu = jnp.minimum(hg, SWIGLU_LIMIT)
        x_lin = jnp.clip(hl, -SWIGLU_LIMIT, SWIGLU_LIMIT)
        act = x_glu * jax.nn.sigmoid(SWIGLU_ALPHA * x_glu) * (x_lin + 1.0)
        contrib = jnp.dot(act.astype(BF16), w2_ref[0], preferred_element_type=F32)
        if first:
            acc[...] = contrib + b2_ref[0]
        else:
            acc[...] += contrib

    @pl.when(jnp.logical_and(active, f == 0))
    def _compute_first():
        compute(True)

    @pl.when(jnp.logical_and(active, f > 0))
    def _compute_rest():
        compute(False)

    @pl.when(jnp.logical_and(jnp.logical_and(active, i == n_blocks - 1), f == n_f - 1))
    def _drain():
        wait_rows(1 - slot)

    @pl.when(jnp.logical_and(active, f == n_f - 1))
    def _scatter():
        n_valid = nval_ref[i]
        lax.fori_loop(0, n_valid, lambda r, c: (row_out(r).start(), c)[1], 0)
        lax.fori_loop(0, n_valid, lambda r, c: (row_out(r).wait(), c)[1], 0)


def _moe_branch(h1, g_moe, w_router, b_router, w1, b1, w2, b2):
    n_tok, d = h1.shape
    n_exp, _, f2 = w1.shape
    fdim = f2 // 2
    bm = _pick(256, n_tok)
    wr = jnp.zeros((d, LANES), F32).at[:, :n_exp].set(w_router)
    br = jnp.zeros((1, LANES), F32).at[0, :n_exp].set(b_router)
    m, route = pl.pallas_call(
        functools.partial(_router_kernel, n_experts=n_exp),
        out_shape=(jax.ShapeDtypeStruct((n_tok, d), F32), jax.ShapeDtypeStruct((n_tok, LANES), F32)),
        grid=(n_tok // bm,),
        in_specs=[pl.BlockSpec((bm, d), lambda i: (i, 0)), pl.BlockSpec((1, d), lambda i: (0, 0)),
                  pl.BlockSpec((d, LANES), lambda i: (0, 0)), pl.BlockSpec((1, LANES), lambda i: (0, 0))],
        out_specs=(pl.BlockSpec((bm, d), lambda i: (i, 0)), pl.BlockSpec((bm, LANES), lambda i: (i, 0))),
        compiler_params=_params(("parallel",)),
        name="moe_router",
    )(h1, g_moe.reshape(1, d), wr, br)
    top_e = route[:, :TOP_K].astype(jnp.int32)

    n_assign = n_tok * TOP_K
    blk = _pick(512, n_assign)
    n_blocks = n_assign // blk + n_exp
    n_rows = n_blocks * blk
    e_flat = top_e.reshape(-1)
    onehot = (e_flat[:, None] == jnp.arange(n_exp, dtype=jnp.int32)[None, :]).astype(jnp.int32)
    csum = jnp.cumsum(onehot, axis=0)
    rank = jnp.take_along_axis(csum, e_flat[:, None], axis=1)[:, 0] - 1
    counts = csum[-1]
    padded = (counts + blk - 1) // blk * blk
    pend = jnp.cumsum(padded)
    pstart = pend - padded
    dest = pstart[e_flat] + rank
    row_a = jnp.zeros((n_rows,), jnp.int32).at[dest].set(jnp.arange(1, n_assign + 1, dtype=jnp.int32)) - 1
    row_tok = jnp.maximum(row_a, 0) // TOP_K
    row_tgt = (jnp.maximum(row_a, 0) % TOP_K) * n_tok + row_tok
    n_act = (pend[-1] // blk).astype(jnp.int32)
    blk_ids = jnp.minimum(jnp.arange(n_blocks, dtype=jnp.int32), n_act - 1)
    blk_e = jnp.minimum(jnp.sum(pend[None, :] <= (blk_ids * blk)[:, None], axis=1), n_exp - 1).astype(jnp.int32)
    n_valid = jnp.clip(pstart[blk_e] + counts[blk_e] - blk_ids * blk, 0, blk).astype(jnp.int32)

    fc = _pick(512, fdim)
    n_f = fdim // fc
    w1g, w1l = _deinterleave(w1)
    b1g = b1[:, 0::2].reshape(n_exp, 1, fdim)
    b1l = b1[:, 1::2].reshape(n_exp, 1, fdim)
    grid_spec = pltpu.PrefetchScalarGridSpec(
        num_scalar_prefetch=3,
        grid=(n_blocks, n_f),
        in_specs=[
            pl.BlockSpec((1, 1, blk), lambda i, f, be, na, nv: (i, 0, 0), memory_space=pltpu.SMEM),
            pl.BlockSpec((1, 1, blk), lambda i, f, be, na, nv: (jnp.minimum(i + 1, n_blocks - 1), 0, 0),
                         memory_space=pltpu.SMEM),
            pl.BlockSpec((1, 1, blk), lambda i, f, be, na, nv: (i, 0, 0), memory_space=pltpu.SMEM),
            pl.BlockSpec(memory_space=pl.ANY),
            pl.BlockSpec((1, d, fc), lambda i, f, be, na, nv: (be[i], 0, f)),
            pl.BlockSpec((1, d, fc), lambda i, f, be, na, nv: (be[i], 0, f)),
            pl.BlockSpec((1, 1, fc), lambda i, f, be, na, nv: (be[i], 0, f)),
            pl.BlockSpec((1, 1, fc), lambda i, f, be, na, nv: (be[i], 0, f)),
            pl.BlockSpec((1, fc, d), lambda i, f, be, na, nv: (be[i], f, 0)),
            pl.BlockSpec((1, 1, d), lambda i, f, be, na, nv: (be[i], 0, 0)),
        ],
        out_specs=pl.BlockSpec(memory_space=pl.ANY),
        scratch_shapes=[pltpu.VMEM((2, blk, d), F32), pltpu.VMEM((blk, d), F32),
                        pltpu.SemaphoreType.DMA((2,)), pltpu.SemaphoreType.DMA],
    )
    y4 = pl.pallas_call(
        functools.partial(_moe_kernel, blk=blk, n_f=n_f, n_blocks=n_blocks),
        out_shape=jax.ShapeDtypeStruct((n_assign, d), F32),
        grid_spec=grid_spec,
        compiler_params=pltpu.CompilerParams(dimension_semantics=("arbitrary", "arbitrary"),
                                             vmem_limit_bytes=VMEM_LIMIT, has_side_effects=True),
        name="moe_experts",
    )(blk_e, n_act.reshape(1), n_valid, row_tok.reshape(n_blocks, 1, blk), row_tok.reshape(n_blocks, 1, blk),
      row_tgt.reshape(n_blocks, 1, blk),
      m, w1g, w1l, b1g, b1l, w2.astype(BF16), b2.reshape(n_exp, 1, d))
    return y4, route


def _combine_kernel(h_ref, y0_ref, y1_ref, y2_ref, y3_ref, r_ref, g_ref, h2_ref, n2_ref):
    r = r_ref[...]
    gk = [r[:, TOP_K + k:TOP_K + k + 1] for k in range(TOP_K)]
    h2 = h_ref[...] + ((gk[0] * y0_ref[...] + gk[1] * y1_ref[...]) + (gk[2] * y2_ref[...] + gk[3] * y3_ref[...]))
    h2_ref[...] = h2
    ms = jnp.mean(h2 * h2, axis=-1, keepdims=True)
    n2_ref[...] = (h2 * lax.rsqrt(ms + EPS) * g_ref[...]).astype(n2_ref.dtype)


def _combine(h1, y4, route, g_ple):
    n_tok, d = h1.shape
    bm = _pick(128, n_tok)
    nb = n_tok // bm
    yspec = [pl.BlockSpec((bm, d), functools.partial(lambda i, k: (k * nb + i, 0), k=k)) for k in range(TOP_K)]
    return pl.pallas_call(
        _combine_kernel,
        out_shape=(jax.ShapeDtypeStruct((n_tok, d), F32), jax.ShapeDtypeStruct((n_tok, d), BF16)),
        grid=(nb,),
        in_specs=[pl.BlockSpec((bm, d), lambda i: (i, 0))] + yspec
        + [pl.BlockSpec((bm, LANES), lambda i: (i, 0)), pl.BlockSpec((1, d), lambda i: (0, 0))],
        out_specs=(pl.BlockSpec((bm, d), lambda i: (i, 0)), pl.BlockSpec((bm, d), lambda i: (i, 0))),
        compiler_params=_params(("parallel",)),
        name="moe_combine",
    )(h1, y4, y4, y4, y4, route, g_ple.reshape(1, d))


def _rope_tables(seq, rot_dim, width):
    half = rot_dim // 2
    inv = ROPE_THETA ** (-jnp.arange(half, dtype=F32) / half)
    ang = jnp.arange(seq, dtype=F32)[:, None] * inv[None, :]
    return jnp.cos(ang), jnp.sin(ang), half


def _layer(h, p_i, g_mix, w_in, b_gate, lam_re, lam_im, log_dt, b_re, b_im, c_re, c_im, d_skip, w_glu,
           w_ssm_out, w_attn_out, w_o, g_moe, w_router, b_router, w1, b1, w2, b2, g_ple, w_ple,
           w_ple_gate, b_ple_gate, bsz, seq):
    n_tok, d = h.shape
    ssm_w = w_glu.shape[0]
    aw = w_attn_out.shape[0]
    n_heads = aw // HEAD_DIM
    n_idx = (w_in.shape[1] - ssm_w - 3 * aw - IDX_DIM - 2 * d) // (IDX_DIM + 1)
    o_u, o_qkv, o_qi, o_ki, o_wi, o_gt = np.cumsum([0, ssm_w, 3 * aw, n_idx * IDX_DIM, IDX_DIM, n_idx]).tolist()
    bm_t = _pick(1024, seq)
    nbt = seq // bm_t

    a = _rmsnorm(h, g_mix, BF16, "rms_mix")
    w_a16 = w_in[:, :o_ki + 2 * IDX_DIM].astype(BF16)
    w_g16 = w_in[:, o_gt:].astype(BF16)

    u = _matmul([(a, w_a16, o_u)], [], _epi_plain, ssm_w, F32, 1024, 1024, "proj_u")

    cos, sin, _ = _rope_tables(seq, HEAD_DIM, HEAD_DIM)
    cos_f = jnp.concatenate([cos, cos], axis=1)
    sin_f = jnp.concatenate([-sin, sin], axis=1)
    tab = lambda t: (t, (bm_t, LANES), lambda i, j: (i % nbt, 0))
    qk = _matmul([(a, w_a16, o_qkv)], [tab(cos_f), tab(sin_f)], _epi_rope_full,
                 2 * aw, BF16, bm_t, 1024, "proj_qk")
    v = _matmul([(a, w_a16, o_qkv + 2 * aw)], [], _epi_plain, aw, BF16, 1024, 1024, "proj_v")

    cos_i, sin_i, half = _rope_tables(seq, IDX_ROPE_DIM, IDX_DIM)
    zeros = jnp.zeros((seq, half), F32)
    cos_i = jnp.concatenate([cos_i, cos_i, jnp.ones((seq, IDX_DIM - 2 * half), F32)], axis=1)
    sin_a = jnp.concatenate([-sin_i, zeros, jnp.zeros((seq, IDX_DIM - 2 * half), F32)], axis=1)
    sin_b = jnp.concatenate([zeros, sin_i, jnp.zeros((seq, IDX_DIM - 2 * half), F32)], axis=1)
    itabs = [tab(cos_i), tab(sin_a), tab(sin_b)]
    qi = _matmul([(a, w_a16, o_qi)], itabs, _epi_rope_idx, n_idx * IDX_DIM, BF16, bm_t, 1024, "proj_qi",
                 head_major=True)
    kiwi = _matmul([(a, w_a16, o_ki)], itabs, _epi_kiwi, 2 * IDX_DIM, F32, bm_t, 2 * IDX_DIM, "proj_kiwi")

    gates = _matmul([(a, w_g16)],
                    [(b_gate.reshape(1, 2 * d), (1, _pick(1024, 2 * d)), lambda i, j: (0, j))],
                    _epi_sigmoid_bias, 2 * d, BF16, 1024, 1024, "proj_gates")

    g_ssm = _s5_branch(u, bsz, seq, lam_re, lam_im, log_dt, b_re, b_im, c_re, c_im, d_skip)
    bn_g = _pick(1024, ssm_w)
    y_ssm = _matmul([(g_ssm, w_glu.astype(BF16))], [(g_ssm, (_pick(1024, n_tok), bn_g), lambda i, j: (i, j))],
                    _epi_glu, ssm_w, BF16, 1024, 1024, "ssm_glu")

    y_att = _dsa_branch(qk, v, qi, kiwi, bsz, seq, n_heads, n_idx)

    bm_m, bn_m = _pick(512, n_tok), _pick(1024, d)
    nj = d // bn_m
    merged = _matmul([(y_ssm, w_ssm_out.astype(BF16)), (y_att, w_attn_out.astype(BF16))],
                     [(gates, (bm_m, bn_m), lambda i, j: (i, j)), (gates, (bm_m, bn_m), lambda i, j: (i, nj + j))],
                     _epi_merge, d, BF16, bm_m, bn_m, "merge")
    h1 = _matmul([(merged, w_o.astype(BF16))], [(h, (bm_m, bn_m), lambda i, j: (i, j))],
                 _epi_residual, d, F32, bm_m, bn_m, "out_proj")

    y4, route = _moe_branch(h1, g_moe, w_router, b_router, w1, b1, w2, b2)
    h2, n2 = _combine(h1, y4, route, g_ple)

    h3 = _matmul([(n2, w_ple_gate.astype(BF16)), (p_i.astype(BF16), w_ple.astype(BF16))],
                 [(h2, (bm_m, bn_m), lambda i, j: (i, j)), (b_ple_gate.reshape(1, d), (1, bn_m), lambda i, j: (0, j))],
                 _epi_ple, d, F32, bm_m, bn_m, "ple")
    return h3


def kernel(x, p, g_mix, w_in, b_gate, ssm_lambda_re, ssm_lambda_im, ssm_log_dt, ssm_b_re, ssm_b_im, ssm_c_re, ssm_c_im, ssm_d, ssm_w_glu, w_ssm_out, w_attn_out, w_o, g_moe, w_router, b_router, w_expert_in, b_expert_in, w_expert_out, b_expert_out, g_ple, w_ple, w_ple_gate, b_ple_gate, g_final):
    bsz, seq, d = x.shape
    h = x.reshape(bsz * seq, d)
    for i in range(p.shape[0]):
        h = _layer(h, p[i].reshape(bsz * seq, -1), g_mix[i], w_in[i], b_gate[i], ssm_lambda_re[i],
                   ssm_lambda_im[i], ssm_log_dt[i], ssm_b_re[i], ssm_b_im[i], ssm_c_re[i], ssm_c_im[i],
                   ssm_d[i], ssm_w_glu[i], w_ssm_out[i], w_attn_out[i], w_o[i], g_moe[i], w_router[i],
                   b_router[i], w_expert_in[i], b_expert_in[i], w_expert_out[i], b_expert_out[i],
                   g_ple[i], w_ple[i], w_ple_gate[i], b_ple_gate[i], bsz, seq)
    return _rmsnorm(h, g_final, F32, "rms_final").reshape(bsz, seq, d)
```

```python
import functools
import math

import jax
import jax.numpy as jnp
import numpy as np
from jax import lax
from jax.experimental import pallas as pl
from jax.experimental.pallas import tpu as pltpu

EPS = 1e-6
HEAD_DIM = 128
IDX_DIM = 128
IDX_ROPE_DIM = 64
TOPK_MAX = 256
ROPE_THETA = 10000.0
TOP_K = 4
SWIGLU_ALPHA = 1.702
SWIGLU_LIMIT = 7.0
LANES = 128
MXU_DIM = 256
VMEM_LIMIT = 56 * 1024 * 1024
NEG_BIG = -1e30
ATTN_HEADS_PER_STEP = 2
ATTN_KEY_CHUNK = 1024
ATTN_Q_BLOCK = 512
INDEX_Q_BLOCK = 256
S5_BATCH_BLOCK = 2

F32 = jnp.float32
BF16 = jnp.bfloat16


def _pick(pref, n):
    b = min(pref, n)
    while n % b:
        b -= 1
    return b


def _params(sem):
    return pltpu.CompilerParams(dimension_semantics=sem, vmem_limit_bytes=VMEM_LIMIT)


def _rms_kernel(x_ref, g_ref, o_ref):
    x = x_ref[...]
    ms = jnp.mean(x * x, axis=-1, keepdims=True)
    o_ref[...] = (x * lax.rsqrt(ms + EPS) * g_ref[...]).astype(o_ref.dtype)


def _rmsnorm(x2d, g, out_dtype, name):
    m, d = x2d.shape
    bm = _pick(256, m)
    return pl.pallas_call(
        _rms_kernel,
        out_shape=jax.ShapeDtypeStruct((m, d), out_dtype),
        grid=(m // bm,),
        in_specs=[pl.BlockSpec((bm, d), lambda i: (i, 0)), pl.BlockSpec((1, d), lambda i: (0, 0))],
        out_specs=pl.BlockSpec((bm, d), lambda i: (i, 0)),
        compiler_params=_params(("parallel",)),
        name=name,
    )(x2d, g.reshape(1, d))


def _mm_kernel(*refs, n_lhs, n_extra, epilogue):
    lhs = refs[: 2 * n_lhs]
    extras = refs[2 * n_lhs: 2 * n_lhs + n_extra]
    o_ref = refs[2 * n_lhs + n_extra]
    accs = [jnp.dot(lhs[2 * t][...], lhs[2 * t + 1][...], preferred_element_type=F32) for t in range(n_lhs)]
    epilogue(accs, extras, o_ref)


def _matmul(pairs, extras, epilogue, n_out, out_dtype, bm, bn, name, head_major=False):
    m = pairs[0][0].shape[0]
    bm = _pick(bm, m)
    bn = _pick(bn, n_out)
    in_specs, args = [], []
    for pair in pairs:
        a, w = pair[:2]
        off = pair[2] if len(pair) > 2 else 0
        if off % bn:
            w, off = w[:, off:off + n_out], 0
        k = a.shape[1]
        in_specs += [pl.BlockSpec((bm, k), lambda i, j: (i, 0)),
                     pl.BlockSpec((k, bn), functools.partial(lambda i, j, ob: (0, j + ob), ob=off // bn))]
        args += [a, w]
    for arr, bshape, imap in extras:
        in_specs.append(pl.BlockSpec(bshape, imap))
        args.append(arr)
    if head_major:
        out_shape = jax.ShapeDtypeStruct((n_out // LANES, m, LANES), out_dtype)
        out_spec = pl.BlockSpec((bn // LANES, bm, LANES), lambda i, j: (j, i, 0))
    else:
        out_shape = jax.ShapeDtypeStruct((m, n_out), out_dtype)
        out_spec = pl.BlockSpec((bm, bn), lambda i, j: (i, j))
    return pl.pallas_call(
        functools.partial(_mm_kernel, n_lhs=len(pairs), n_extra=len(extras), epilogue=epilogue),
        out_shape=out_shape,
        grid=(m // bm, n_out // bn),
        in_specs=in_specs,
        out_specs=out_spec,
        compiler_params=_params(("parallel", "arbitrary")),
        name=name,
    )(*args)


def _epi_plain(accs, extras, o_ref):
    o_ref[...] = accs[0].astype(o_ref.dtype)


def _epi_rope_full(accs, extras, o_ref):
    cos, sin = extras[0][...], extras[1][...]
    acc = accs[0]
    for h in range(acc.shape[1] // HEAD_DIM):
        xh = acc[:, h * HEAD_DIM:(h + 1) * HEAD_DIM]
        o_ref[:, h * HEAD_DIM:(h + 1) * HEAD_DIM] = (
            xh * cos + pltpu.roll(xh, HEAD_DIM // 2, 1) * sin).astype(o_ref.dtype)


def _rope_idx(xh, cos, sin_a, sin_b):
    half = IDX_ROPE_DIM // 2
    return xh * cos + pltpu.roll(xh, IDX_DIM - half, 1) * sin_a + pltpu.roll(xh, half, 1) * sin_b


def _epi_rope_idx(accs, extras, o_ref):
    cos, sin_a, sin_b = extras[0][...], extras[1][...], extras[2][...]
    acc = accs[0]
    for h in range(acc.shape[1] // IDX_DIM):
        xh = acc[:, h * IDX_DIM:(h + 1) * IDX_DIM]
        o_ref[h] = _rope_idx(xh, cos, sin_a, sin_b).astype(o_ref.dtype)


def _epi_kiwi(accs, extras, o_ref):
    cos, sin_a, sin_b = extras[0][...], extras[1][...], extras[2][...]
    acc = accs[0]
    o_ref[:, :IDX_DIM] = _rope_idx(acc[:, :IDX_DIM], cos, sin_a, sin_b).astype(o_ref.dtype)
    o_ref[:, IDX_DIM:] = acc[:, IDX_DIM:].astype(o_ref.dtype)


def _epi_sigmoid_bias(accs, extras, o_ref):
    o_ref[...] = jax.nn.sigmoid(accs[0] + extras[0][...]).astype(o_ref.dtype)


def _epi_glu(accs, extras, o_ref):
    g = extras[0][...].astype(F32)
    o_ref[...] = (g * jax.nn.sigmoid(accs[0])).astype(o_ref.dtype)


def _epi_merge(accs, extras, o_ref):
    g0 = extras[0][...].astype(F32)
    g1 = extras[1][...].astype(F32)
    o_ref[...] = (g0 * accs[0] + g1 * accs[1]).astype(o_ref.dtype)


def _epi_residual(accs, extras, o_ref):
    o_ref[...] = (extras[0][...] + accs[0]).astype(o_ref.dtype)


def _epi_ple(accs, extras, o_ref):
    h, b = extras[0][...], extras[1][...]
    o_ref[...] = (h + accs[1] * jax.nn.sigmoid(accs[0] + b)).astype(o_ref.dtype)


def _s5_kernel(u_ref, w1_ref, w2_ref, w3_ref, ar_ref, ai_ref, d_ref, o_ref, y_scr, w1b, w2b, w3b,
               *, n_chunks, chunk):
    gpb, lp, _ = w1_ref.shape
    p = lp // chunk
    n = w2_ref.shape[2] // 2
    per_half = LANES // p

    @pl.when(pl.program_id(1) == 0)
    def _expand_weights():
        lane = lax.broadcasted_iota(jnp.int32, (1, LANES), 1)
        w2b[...] = jnp.zeros(w2b.shape, BF16)

        def group(gl, carry):
            w1g, w2g, w3g = w1_ref[gl], w2_ref[gl], w3_ref[gl]
            row0 = gl * p
            in_group = lane // p == gl
            for t in range(chunk):
                hf, t8 = divmod(t, per_half)
                shift = (gl * p - t8 * p) % LANES
                src = slice(hf * LANES, (hf + 1) * LANES)
                dst = slice(t * LANES, (t + 1) * LANES)
                p1 = jnp.where(in_group, pltpu.roll(w1g[:, src], shift, 1), 0.0).astype(BF16)
                for s in range(chunk):
                    w1b[pl.ds(pl.multiple_of(s * LANES + row0, p), p), dst] = p1[s * p:(s + 1) * p, :]
                p3 = jnp.where(in_group, pltpu.roll(w3g[:, src], shift, 1), 0.0).astype(BF16)
                for r in range(2):
                    w3b[pl.ds(pl.multiple_of(r * gpb * n + gl * n, n), n), dst] = p3[r * n:(r + 1) * n, :]
            for r in range(2):
                col = r * gpb * n + gl * n
                cb = pl.multiple_of(col // LANES * LANES, LANES)
                p2 = jnp.where(lane // n == (col % LANES) // n,
                               pltpu.roll(w2g, (col - r * n) % LANES, 1), 0.0).astype(BF16)
                for s in range(chunk):
                    w2b[pl.ds(pl.multiple_of(s * LANES + row0, p), p), pl.ds(cb, LANES)] = p2[s * p:(s + 1) * p, :]
            return carry

        lax.fori_loop(0, gpb, group, 0)

    rows = u_ref.shape[0] // chunk
    us = [u_ref[pl.ds(s, rows, stride=chunk), :] for s in range(chunk)]
    u = jnp.concatenate(us, axis=1).astype(BF16)
    x = jnp.dot(u, w2b[...], preferred_element_type=F32)
    cidx = lax.broadcasted_iota(jnp.int32, (rows, 1), 0) % n_chunks
    half = x.shape[1] // 2
    for k in range(int(math.log2(n_chunks))):
        sh = 1 << k
        s = jnp.where(cidx >= sh, pltpu.roll(x, sh, 0), 0.0)
        x = x + ar_ref[0, k:k + 1, :] * s + ai_ref[0, k:k + 1, :] * pltpu.roll(s, half, 1)
    xprev = jnp.where(cidx >= 1, pltpu.roll(x, 1, 0), 0.0)
    y = (jnp.dot(u, w1b[...], preferred_element_type=F32)
         + jnp.dot(xprev.astype(BF16), w3b[...], preferred_element_type=F32))
    d = d_ref[...]
    for s in range(chunk):
        y_scr[pl.ds(s, rows, stride=chunk), :] = jax.nn.gelu(y[:, s * LANES:(s + 1) * LANES] + d * us[s])
    o_ref[...] = y_scr[...].astype(o_ref.dtype)


def _s5_weights(lam_re, lam_im, log_dt, b_re, b_im, c_re, c_im, chunk, n_chunks):
    hi = lax.Precision.HIGHEST
    g, n, p = b_re.shape
    dt = jnp.exp(log_dt)[:, None]
    mag = jnp.exp(lam_re * dt)
    lb_re, lb_im = mag * jnp.cos(lam_im * dt), mag * jnp.sin(lam_im * dt)
    den = lam_re * lam_re + lam_im * lam_im
    nr, ni = lb_re - 1.0, lb_im
    f_re = (nr * lam_re + ni * lam_im) / den
    f_im = (ni * lam_re - nr * lam_im) / den
    bb_re = f_re[..., None] * b_re - f_im[..., None] * b_im
    bb_im = f_re[..., None] * b_im + f_im[..., None] * b_re

    def power(d):
        m = jnp.exp(lam_re * dt * d[:, None, None])
        ang = lam_im * dt * d[:, None, None]
        return m * jnp.cos(ang), m * jnp.sin(ang)

    pw_re, pw_im = power(jnp.arange(chunk + 1, dtype=F32))
    cp_re = c_re[None] * pw_re[:, :, None, :] - c_im[None] * pw_im[:, :, None, :]
    cp_im = c_re[None] * pw_im[:, :, None, :] + c_im[None] * pw_re[:, :, None, :]
    kd = (jnp.einsum('dgpn,gnj->dgpj', cp_re[:chunk], bb_re, precision=hi)
          - jnp.einsum('dgpn,gnj->dgpj', cp_im[:chunk], bb_im, precision=hi))
    s_idx = np.arange(chunk)[:, None]
    t_idx = np.arange(chunk)[None, :]
    lag = np.clip(t_idx - s_idx, 0, chunk - 1)
    w1 = jnp.where((t_idx >= s_idx)[:, :, None, None, None], kd[lag], 0.0)
    w1 = w1.transpose(2, 0, 4, 1, 3).reshape(g, chunk * p, chunk * p)
    rev_re, rev_im = pw_re[:chunk][::-1], pw_im[:chunk][::-1]
    z_re = rev_re[..., None] * bb_re[None] - rev_im[..., None] * bb_im[None]
    z_im = rev_re[..., None] * bb_im[None] + rev_im[..., None] * bb_re[None]
    w2 = jnp.concatenate([z_re, z_im], axis=2).transpose(1, 0, 3, 2).reshape(g, chunk * p, 2 * n)
    w3 = jnp.concatenate([cp_re[1:], -cp_im[1:]], axis=3)
    w3 = w3.transpose(1, 3, 0, 2).reshape(g, 2 * n, chunk * p)
    n_steps = int(math.log2(n_chunks))
    a_re, a_im = power(chunk * (2.0 ** jnp.arange(n_steps, dtype=F32)))
    ar = jnp.concatenate([a_re, a_re], axis=2).transpose(1, 0, 2)
    ai = jnp.concatenate([-a_im, a_im], axis=2).transpose(1, 0, 2)
    gpb = LANES // p
    sg = g // gpb
    arb = ar.reshape(sg, gpb, n_steps, 2, n).transpose(0, 2, 3, 1, 4).reshape(sg, n_steps, 2 * gpb * n)
    aib = ai.reshape(sg, gpb, n_steps, 2, n).transpose(0, 2, 3, 1, 4).reshape(sg, n_steps, 2 * gpb * n)
    return w1, w2, w3, arb, aib


def _s5_branch(u2d, bsz, seq, lam_re, lam_im, log_dt, b_re, b_im, c_re, c_im, d_skip):
    g, n, p = b_re.shape
    chunk = MXU_DIM // p
    n_chunks = seq // chunk
    assert n_chunks & (n_chunks - 1) == 0 and 2 * n == LANES
    w1, w2, w3, ar, ai = _s5_weights(lam_re, lam_im, log_dt, b_re, b_im, c_re, c_im, chunk, n_chunks)
    n_steps = ar.shape[1]
    gpb = LANES // p
    assert g % gpb == 0
    n_tok = bsz * seq
    bb = _pick(S5_BATCH_BLOCK, bsz)
    trows = bb * seq
    lp = chunk * p
    lw, sw = chunk * LANES, 2 * gpb * n
    return pl.pallas_call(
        functools.partial(_s5_kernel, n_chunks=n_chunks, chunk=chunk),
        out_shape=jax.ShapeDtypeStruct((n_tok, g * p), BF16),
        grid=(g // gpb, bsz // bb),
        in_specs=[
            pl.BlockSpec((trows, LANES), lambda i, b: (b, i)),
            pl.BlockSpec((gpb, lp, lp), lambda i, b: (i, 0, 0)),
            pl.BlockSpec((gpb, lp, 2 * n), lambda i, b: (i, 0, 0)),
            pl.BlockSpec((gpb, 2 * n, lp), lambda i, b: (i, 0, 0)),
            pl.BlockSpec((1, n_steps, sw), lambda i, b: (i, 0, 0)),
            pl.BlockSpec((1, n_steps, sw), lambda i, b: (i, 0, 0)),
            pl.BlockSpec((1, LANES), lambda i, b: (0, i)),
        ],
        out_specs=pl.BlockSpec((trows, LANES), lambda i, b: (b, i)),
        scratch_shapes=[pltpu.VMEM((trows, LANES), F32), pltpu.VMEM((lw, lw), BF16), pltpu.VMEM((lw, sw), BF16),
                        pltpu.VMEM((sw, lw), BF16)],
        compiler_params=_params(("parallel", "arbitrary")),
        name="s5_scan",
    )(u2d, w1, w2, w3, ar, ai, d_skip.reshape(1, g * p))


def _dsa_index_kernel(qi_ref, ki_ref, wi_ref, o_ref, key_ref, *, n_heads, topk, scale, kc):
    qb, seq = o_ref.shape
    t0 = pl.program_id(1) * qb
    n_kc = (t0 + qb + kc - 1) // kc
    w = wi_ref[...] * scale
    wcols = [w[:, h:h + 1] for h in range(n_heads)]
    row = t0 + lax.broadcasted_iota(jnp.int32, (qb, 1), 0)

    def chunk_causal(c):
        return c * kc + lax.broadcasted_iota(jnp.int32, (1, kc), 1) <= row

    def scores(c, carry):
        off = pl.multiple_of(c * kc, kc)
        kchunk = ki_ref[pl.ds(off, kc), :]
        s = jnp.zeros((qb, kc), F32)
        for h in range(n_heads):
            z = lax.dot_general(qi_ref[h], kchunk, (((1,), (1,)), ((), ())), preferred_element_type=F32)
            s = s + wcols[h] * jnp.maximum(z, 0.0)
        s = jnp.where(chunk_causal(c), s, -jnp.inf)
        bits = lax.bitcast_convert_type(s, jnp.int32)
        key_ref[:, pl.ds(off, kc)] = jnp.where(bits < 0, bits ^ jnp.int32(0x7FFFFFFF), bits)
        return carry

    lax.fori_loop(0, n_kc, scores, 0)

    def bisect(it, thr):
        cand = thr + lax.shift_left(jnp.int32(1), jnp.int32(31) - jnp.asarray(it, jnp.int32))

        def count(c, acc):
            k = key_ref[:, pl.ds(pl.multiple_of(c * kc, kc), kc)]
            hit = jnp.where(k >= cand, 1.0, 0.0)
            for j in range(kc // LANES):
                acc = acc + hit[:, j * LANES:(j + 1) * LANES]
            return acc

        acc = lax.fori_loop(0, n_kc, count, jnp.zeros((qb, LANES), F32))
        cnt = jnp.sum(acc, axis=1, keepdims=True)
        return jnp.where(cnt >= topk, cand, thr)

    thr = lax.fori_loop(0, 32, bisect, jnp.full((qb, 1), jnp.iinfo(jnp.int32).min, jnp.int32))

    o_ref[...] = jnp.full(o_ref.shape, NEG_BIG, o_ref.dtype)

    def count_ge(cand):
        def count(c, acc):
            hit = jnp.where(key_ref[:, pl.ds(pl.multiple_of(c * kc, kc), kc)] >= cand, 1.0, 0.0)
            for j in range(kc // LANES):
                acc = acc + hit[:, j * LANES:(j + 1) * LANES]
            return acc
        return jnp.sum(lax.fori_loop(0, n_kc, count, jnp.zeros((qb, LANES), F32)), axis=1, keepdims=True)

    key_ninf = jnp.int32(np.array(-np.inf, np.float32).view(np.int32) ^ 0x7FFFFFFF)
    tied = jnp.where(thr > key_ninf, count_ge(thr), 0.0) > topk
    any_tied = jnp.max(jnp.where(tied, 1.0, 0.0)) > 0.0

    @pl.when(jnp.logical_not(any_tied))
    def _emit():
        def emit(c, carry):
            off = pl.multiple_of(c * kc, kc)
            k = key_ref[:, pl.ds(off, kc)]
            bias = jnp.where(k >= thr, jnp.where(chunk_causal(c), 0.0, NEG_BIG), NEG_BIG)
            o_ref[:, pl.ds(off, kc)] = bias.astype(o_ref.dtype)
            return carry

        lax.fori_loop(0, n_kc, emit, 0)

    @pl.when(any_tied)
    def _emit_tied():
        need = topk - count_ge(thr + 1)
        before = jnp.where(lax.broadcasted_iota(jnp.int32, (kc, kc), 0) < lax.broadcasted_iota(jnp.int32, (kc, kc), 1),
                           1.0, 0.0).astype(BF16)

        def emit(c, seen):
            off = pl.multiple_of(c * kc, kc)
            k = key_ref[:, pl.ds(off, kc)]
            eq = jnp.where(k == thr, 1.0, 0.0)
            pos = seen + jnp.dot(eq.astype(BF16), before, preferred_element_type=F32)
            allowed = jnp.where(chunk_causal(c), 0.0, NEG_BIG)
            bias = jnp.where(k > thr, allowed, jnp.where(k == thr, jnp.where(pos < need, allowed, NEG_BIG), NEG_BIG))
            o_ref[:, pl.ds(off, kc)] = bias.astype(o_ref.dtype)
            return seen + jnp.sum(eq, axis=1, keepdims=True)

        lax.fori_loop(0, n_kc, emit, jnp.zeros((qb, 1), F32))


def _dsa_attn_kernel(q_ref, k_ref, v_ref, m_ref, o_ref, *, scale, kc):
    qa = q_ref.shape[0]
    n_h = q_ref.shape[1] // HEAD_DIM
    n_kc = ((pl.program_id(2) + 1) * qa + kc - 1) // kc
    qs = [(q_ref[:, h * HEAD_DIM:(h + 1) * HEAD_DIM].astype(F32) * (scale * math.log2(math.e))).astype(BF16)
          for h in range(n_h)]

    def chunk(c, carry):
        off = pl.multiple_of(c * kc, kc)
        bias = m_ref[:, pl.ds(off, kc)].astype(F32)
        out = []
        for h in range(n_h):
            m, l, acc = carry[h]
            hs = slice(h * HEAD_DIM, (h + 1) * HEAD_DIM)
            s = lax.dot_general(qs[h], k_ref[pl.ds(off, kc), hs], (((1,), (1,)), ((), ())),
                                preferred_element_type=F32) + bias
            m_new = jnp.maximum(m, jnp.max(s, axis=1, keepdims=True))
            alpha = jnp.exp2(m - m_new)
            p = jnp.exp2(s - m_new)
            l = alpha * l + jnp.sum(p, axis=1, keepdims=True)
            acc = alpha * acc + jnp.dot(p.astype(BF16), v_ref[pl.ds(off, kc), hs], preferred_element_type=F32)
            out.append((m_new, l, acc))
        return tuple(out)

    init = tuple((jnp.full((qa, 1), -jnp.inf, F32), jnp.zeros((qa, 1), F32), jnp.zeros((qa, HEAD_DIM), F32))
                 for _ in range(n_h))
    res = lax.fori_loop(0, n_kc, chunk, init)
    for h in range(n_h):
        _, l, acc = res[h]
        o_ref[:, h * HEAD_DIM:(h + 1) * HEAD_DIM] = (acc / l).astype(o_ref.dtype)


def _dsa_branch(qk, v, qi, kiwi, bsz, seq, n_heads, n_idx_heads):
    mask = _dsa_mask(qi, kiwi, bsz, seq, n_idx_heads)
    return _dsa_attend(qk, v, mask, bsz, seq, n_heads)


def _dsa_mask(qi, kiwi, bsz, seq, n_idx_heads):
    n_tok = bsz * seq
    topk = min(TOPK_MAX, seq // 4)
    qb = _pick(INDEX_Q_BLOCK, seq)
    nq = seq // qb
    return pl.pallas_call(
        functools.partial(_dsa_index_kernel, n_heads=n_idx_heads, topk=float(topk),
                          scale=(IDX_DIM ** -0.5) * (n_idx_heads ** -0.5), kc=_pick(MXU_DIM, seq)),
        out_shape=jax.ShapeDtypeStruct((n_tok, seq), BF16),
        grid=(bsz, nq),
        in_specs=[
            pl.BlockSpec((n_idx_heads, qb, IDX_DIM), lambda b, i: (0, b * nq + i, 0)),
            pl.BlockSpec((seq, IDX_DIM), lambda b, i: (b, 0)),
            pl.BlockSpec((qb, IDX_DIM), lambda b, i: (b * nq + i, 1)),
        ],
        out_specs=pl.BlockSpec((qb, seq), lambda b, i: (b * nq + i, 0)),
        scratch_shapes=[pltpu.VMEM((qb, seq), jnp.int32)],
        compiler_params=_params(("parallel", "arbitrary")),
        name="dsa_index",
    )(qi, kiwi.astype(BF16), kiwi)


def _dsa_attend(qk, v, mask, bsz, seq, n_heads):
    n_tok = bsz * seq
    aw = n_heads * HEAD_DIM
    qa = _pick(ATTN_Q_BLOCK, seq)
    na = seq // qa
    hp = ATTN_HEADS_PER_STEP
    hw = hp * HEAD_DIM
    ng = n_heads // hp
    return pl.pallas_call(
        functools.partial(_dsa_attn_kernel, scale=HEAD_DIM ** -0.5, kc=_pick(ATTN_KEY_CHUNK, seq)),
        out_shape=jax.ShapeDtypeStruct((n_tok, aw), BF16),
        grid=(bsz, ng, na),
        in_specs=[
            pl.BlockSpec((qa, hw), lambda b, h, i: (b * na + i, h)),
            pl.BlockSpec((seq, hw), lambda b, h, i: (b, ng + h)),
            pl.BlockSpec((seq, hw), lambda b, h, i: (b, h)),
            pl.BlockSpec((qa, seq), lambda b, h, i: (b * na + i, 0)),
        ],
        out_specs=pl.BlockSpec((qa, hw), lambda b, h, i: (b * na + i, h)),
        compiler_params=_params(("parallel", "parallel", "arbitrary")),
        name="dsa_attn",
    )(qk, qk, v, mask)


def _router_kernel(h_ref, g_ref, wr_ref, br_ref, m_ref, r_ref, *, n_experts):
    x = h_ref[...]
    ms = jnp.mean(x * x, axis=-1, keepdims=True)
    m = x * lax.rsqrt(ms + EPS) * g_ref[...]
    m_ref[...] = m
    logits = jnp.dot(m, wr_ref[...], precision=lax.Precision.HIGHEST, preferred_element_type=F32) + br_ref[...]
    lane = lax.broadcasted_iota(jnp.int32, logits.shape, 1)
    l = jnp.where(lane < n_experts, logits, -jnp.inf)
    vals, idxs = [], []
    for _ in range(TOP_K):
        mx = jnp.max(l, axis=1, keepdims=True)
        am = jnp.min(jnp.where(l == mx, lane, LANES), axis=1, keepdims=True)
        vals.append(mx)
        idxs.append(am)
        l = jnp.where(lane == am, -jnp.inf, l)
    exps = [jnp.exp(v - vals[0]) for v in vals]
    tot = exps[0] + exps[1] + exps[2] + exps[3]
    out = jnp.zeros(logits.shape, F32)
    for k in range(TOP_K):
        out = jnp.where(lane == k, idxs[k].astype(F32), out)
        out = jnp.where(lane == TOP_K + k, exps[k] / tot, out)
    r_ref[...] = out


def _deinterleave_kernel(w_ref, p_ref, g_ref, l_ref):
    w = w_ref[0].astype(BF16)
    for c in range(w.shape[1] // MXU_DIM):
        z = jnp.dot(w[:, c * MXU_DIM:(c + 1) * MXU_DIM], p_ref[...], preferred_element_type=F32)
        g_ref[0, :, c * LANES:(c + 1) * LANES] = z[:, :LANES].astype(BF16)
        l_ref[0, :, c * LANES:(c + 1) * LANES] = z[:, LANES:].astype(BF16)


def _deinterleave(w1):
    n_exp, d, f2 = w1.shape
    tk = _pick(512, d)
    perm = np.zeros((MXU_DIM, MXU_DIM), np.float32)
    perm[np.arange(0, MXU_DIM, 2), np.arange(LANES)] = 1.0
    perm[np.arange(1, MXU_DIM, 2), LANES + np.arange(LANES)] = 1.0
    out = jax.ShapeDtypeStruct((n_exp, d, f2 // 2), BF16)
    return pl.pallas_call(
        _deinterleave_kernel,
        out_shape=(out, out),
        grid=(n_exp, d // tk),
        in_specs=[pl.BlockSpec((1, tk, f2), lambda e, k: (e, k, 0)),
                  pl.BlockSpec((MXU_DIM, MXU_DIM), lambda e, k: (0, 0))],
        out_specs=(pl.BlockSpec((1, tk, f2 // 2), lambda e, k: (e, k, 0)),
                   pl.BlockSpec((1, tk, f2 // 2), lambda e, k: (e, k, 0))),
        compiler_params=_params(("parallel", "parallel")),
        name="moe_deinterleave",
    )(w1, jnp.asarray(perm, BF16))


def _moe_kernel(blk_e_ref, nact_ref, nval_ref, tok_ref, tok_next_ref, tgt_ref, m_hbm, w1g_ref, w1l_ref,
                b1g_ref, b1l_ref, w2_ref, b2_ref, y_hbm, xbuf, acc, sem_in, sem_out, *, blk, n_f, n_blocks):
    i = pl.program_id(0)
    f = pl.program_id(1)
    n_act = nact_ref[0]
    active = i < n_act
    slot = i % 2

    def row_in(idx_ref, r, s):
        return pltpu.make_async_copy(m_hbm.at[pl.ds(idx_ref[0, 0, r], 1)], xbuf.at[s, pl.ds(r, 1)], sem_in.at[s])

    def row_out(r):
        return pltpu.make_async_copy(acc.at[pl.ds(r, 1)], y_hbm.at[pl.ds(tgt_ref[0, 0, r], 1)], sem_out)

    def wait_rows(s):
        lax.fori_loop(0, blk, lambda r, c: (row_in(tok_ref, r, s).wait(), c)[1], 0, unroll=8)

    @pl.when(jnp.logical_and(i == 0, f == 0))
    def _first_gather():
        lax.fori_loop(0, blk, lambda r, c: (row_in(tok_ref, r, 0).start(), c)[1], 0, unroll=8)

    @pl.when(jnp.logical_and(i <= n_act, f == 0))
    def _wait_gather():
        wait_rows(slot)

    def compute(first):
        if first:
            for r in range(blk):
                row_in(tok_next_ref, r, 1 - slot).start()
        x = xbuf[slot].astype(BF16)
        hg = jnp.dot(x, w1g_ref[0], preferred_element_type=F32) + b1g_ref[0]
        hl = jnp.dot(x, w1l_ref[0], preferred_element_type=F32) + b1l_ref[0]
        x_glu = jnp.minimum(hg, SWIGLU_LIMIT)
        x_lin = jnp.clip(hl, -SWIGLU_LIMIT, SWIGLU_LIMIT)
        act = x_glu * jax.nn.sigmoid(SWIGLU_ALPHA * x_glu) * (x_lin + 1.0)
        contrib = jnp.dot(act.astype(BF16), w2_ref[0], preferred_element_type=F32)
        if first:
            acc[...] = contrib + b2_ref[0]
        else:
            acc[...] += contrib

    @pl.when(jnp.logical_and(active, f == 0))
    def _compute_first():
        compute(True)

    @pl.when(jnp.logical_and(active, f > 0))
    def _compute_rest():
        compute(False)

    @pl.when(jnp.logical_and(jnp.logical_and(active, i == n_blocks - 1), f == n_f - 1))
    def _drain():
        wait_rows(1 - slot)

    @pl.when(jnp.logical_and(active, f == n_f - 1))
    def _scatter():
        n_valid = nval_ref[i]
        lax.fori_loop(0, n_valid, lambda r, c: (row_out(r).start(), c)[1], 0)
        lax.fori_loop(0, n_valid, lambda r, c: (row_out(r).wait(), c)[1], 0)


def _moe_branch(h1, g_moe, w_router, b_router, w1, b1, w2, b2):
    n_tok, d = h1.shape
    n_exp, _, f2 = w1.shape
    fdim = f2 // 2
    bm = _pick(256, n_tok)
    wr = jnp.zeros((d, LANES), F32).at[:, :n_exp].set(w_router)
    br = jnp.zeros((1, LANES), F32).at[0, :n_exp].set(b_router)
    m, route = pl.pallas_call(
        functools.partial(_router_kernel, n_experts=n_exp),
        out_shape=(jax.ShapeDtypeStruct((n_tok, d), F32), jax.ShapeDtypeStruct((n_tok, LANES), F32)),
        grid=(n_tok // bm,),
        in_specs=[pl.BlockSpec((bm, d), lambda i: (i, 0)), pl.BlockSpec((1, d), lambda i: (0, 0)),
                  pl.BlockSpec((d, LANES), lambda i: (0, 0)), pl.BlockSpec((1, LANES), lambda i: (0, 0))],
        out_specs=(pl.BlockSpec((bm, d), lambda i: (i, 0)), pl.BlockSpec((bm, LANES), lambda i: (i, 0))),
        compiler_params=_params(("parallel",)),
        name="moe_router",
    )(h1, g_moe.reshape(1, d), wr, br)
    top_e = route[:, :TOP_K].astype(jnp.int32)

    n_assign = n_tok * TOP_K
    blk = _pick(512, n_assign)
    n_blocks = n_assign // blk + n_exp
    n_rows = n_blocks * blk
    e_flat = top_e.reshape(-1)
    onehot = (e_flat[:, None] == jnp.arange(n_exp, dtype=jnp.int32)[None, :]).astype(jnp.int32)
    csum = jnp.cumsum(onehot, axis=0)
    rank = jnp.take_along_axis(csum, e_flat[:, None], axis=1)[:, 0] - 1
    counts = csum[-1]
    padded = (counts + blk - 1) // blk * blk
    pend = jnp.cumsum(padded)
    pstart = pend - padded
    dest = pstart[e_flat] + rank
    row_a = jnp.zeros((n_rows,), jnp.int32).at[dest].set(jnp.arange(1, n_assign + 1, dtype=jnp.int32)) - 1
    row_tok = jnp.maximum(row_a, 0) // TOP_K
    row_tgt = (jnp.maximum(row_a, 0) % TOP_K) * n_tok + row_tok
    n_act = (pend[-1] // blk).astype(jnp.int32)
    blk_ids = jnp.minimum(jnp.arange(n_blocks, dtype=jnp.int32), n_act - 1)
    blk_e = jnp.minimum(jnp.sum(pend[None, :] <= (blk_ids * blk)[:, None], axis=1), n_exp - 1).astype(jnp.int32)
    n_valid = jnp.clip(pstart[blk_e] + counts[blk_e] - blk_ids * blk, 0, blk).astype(jnp.int32)

    fc = _pick(512, fdim)
    n_f = fdim // fc
    w1g, w1l = _deinterleave(w1)
    b1g = b1[:, 0::2].reshape(n_exp, 1, fdim)
    b1l = b1[:, 1::2].reshape(n_exp, 1, fdim)
    grid_spec = pltpu.PrefetchScalarGridSpec(
        num_scalar_prefetch=3,
        grid=(n_blocks, n_f),
        in_specs=[
            pl.BlockSpec((1, 1, blk), lambda i, f, be, na, nv: (i, 0, 0), memory_space=pltpu.SMEM),
            pl.BlockSpec((1, 1, blk), lambda i, f, be, na, nv: (jnp.minimum(i + 1, n_blocks - 1), 0, 0),
                         memory_space=pltpu.SMEM),
            pl.BlockSpec((1, 1, blk), lambda i, f, be, na, nv: (i, 0, 0), memory_space=pltpu.SMEM),
            pl.BlockSpec(memory_space=pl.ANY),
            pl.BlockSpec((1, d, fc), lambda i, f, be, na, nv: (be[i], 0, f)),
            pl.BlockSpec((1, d, fc), lambda i, f, be, na, nv: (be[i], 0, f)),
            pl.BlockSpec((1, 1, fc), lambda i, f, be, na, nv: (be[i], 0, f)),
            pl.BlockSpec((1, 1, fc), lambda i, f, be, na, nv: (be[i], 0, f)),
            pl.BlockSpec((1, fc, d), lambda i, f, be, na, nv: (be[i], f, 0)),
            pl.BlockSpec((1, 1, d), lambda i, f, be, na, nv: (be[i], 0, 0)),
        ],
        out_specs=pl.BlockSpec(memory_space=pl.ANY),
        scratch_shapes=[pltpu.VMEM((2, blk, d), F32), pltpu.VMEM((blk, d), F32),
                        pltpu.SemaphoreType.DMA((2,)), pltpu.SemaphoreType.DMA],
    )
    y4 = pl.pallas_call(
        functools.partial(_moe_kernel, blk=blk, n_f=n_f, n_blocks=n_blocks),
        out_shape=jax.ShapeDtypeStruct((n_assign, d), F32),
        grid_spec=grid_spec,
        compiler_params=pltpu.CompilerParams(dimension_semantics=("arbitrary", "arbitrary"),
                                             vmem_limit_bytes=VMEM_LIMIT, has_side_effects=True),
        name="moe_experts",
    )(blk_e, n_act.reshape(1), n_valid, row_tok.reshape(n_blocks, 1, blk), row_tok.reshape(n_blocks, 1, blk),
      row_tgt.reshape(n_blocks, 1, blk),
      m, w1g, w1l, b1g, b1l, w2.astype(BF16), b2.reshape(n_exp, 1, d))
    return y4, route


def _combine_kernel(h_ref, y0_ref, y1_ref, y2_ref, y3_ref, r_ref, g_ref, h2_ref, n2_ref):
    r = r_ref[...]
    gk = [r[:, TOP_K + k:TOP_K + k + 1] for k in range(TOP_K)]
    h2 = h_ref[...] + ((gk[0] * y0_ref[...] + gk[1] * y1_ref[...]) + (gk[2] * y2_ref[...] + gk[3] * y3_ref[...]))
    h2_ref[...] = h2
    ms = jnp.mean(h2 * h2, axis=-1, keepdims=True)
    n2_ref[...] = (h2 * lax.rsqrt(ms + EPS) * g_ref[...]).astype(n2_ref.dtype)


def _combine(h1, y4, route, g_ple):
    n_tok, d = h1.shape
    bm = _pick(128, n_tok)
    nb = n_tok // bm
    yspec = [pl.BlockSpec((bm, d), functools.partial(lambda i, k: (k * nb + i, 0), k=k)) for k in range(TOP_K)]
    return pl.pallas_call(
        _combine_kernel,
        out_shape=(jax.ShapeDtypeStruct((n_tok, d), F32), jax.ShapeDtypeStruct((n_tok, d), BF16)),
        grid=(nb,),
        in_specs=[pl.BlockSpec((bm, d), lambda i: (i, 0))] + yspec
        + [pl.BlockSpec((bm, LANES), lambda i: (i, 0)), pl.BlockSpec((1, d), lambda i: (0, 0))],
        out_specs=(pl.BlockSpec((bm, d), lambda i: (i, 0)), pl.BlockSpec((bm, d), lambda i: (i, 0))),
        compiler_params=_params(("parallel",)),
        name="moe_combine",
    )(h1, y4, y4, y4, y4, route, g_ple.reshape(1, d))


def _rope_tables(seq, rot_dim, width):
    half = rot_dim // 2
    inv = ROPE_THETA ** (-jnp.arange(half, dtype=F32) / half)
    ang = jnp.arange(seq, dtype=F32)[:, None] * inv[None, :]
    return jnp.cos(ang), jnp.sin(ang), half


def _layer(h, p_i, g_mix, w_in, b_gate, lam_re, lam_im, log_dt, b_re, b_im, c_re, c_im, d_skip, w_glu,
           w_ssm_out, w_attn_out, w_o, g_moe, w_router, b_router, w1, b1, w2, b2, g_ple, w_ple,
           w_ple_gate, b_ple_gate, bsz, seq):
    n_tok, d = h.shape
    ssm_w = w_glu.shape[0]
    aw = w_attn_out.shape[0]
    n_heads = aw // HEAD_DIM
    n_idx = (w_in.shape[1] - ssm_w - 3 * aw - IDX_DIM - 2 * d) // (IDX_DIM + 1)
    o_u, o_qkv, o_qi, o_ki, o_wi, o_gt = np.cumsum([0, ssm_w, 3 * aw, n_idx * IDX_DIM, IDX_DIM, n_idx]).tolist()
    bm_t = _pick(1024, seq)
    nbt = seq // bm_t

    a = _rmsnorm(h, g_mix, BF16, "rms_mix")
    w_a16 = w_in[:, :o_ki + 2 * IDX_DIM].astype(BF16)
    w_g16 = w_in[:, o_gt:].astype(BF16)

    u = _matmul([(a, w_a16, o_u)], [], _epi_plain, ssm_w, F32, 1024, 1024, "proj_u")

    cos, sin, _ = _rope_tables(seq, HEAD_DIM, HEAD_DIM)
    cos_f = jnp.concatenate([cos, cos], axis=1)
    sin_f = jnp.concatenate([-sin, sin], axis=1)
    tab = lambda t: (t, (bm_t, LANES), lambda i, j: (i % nbt, 0))
    qk = _matmul([(a, w_a16, o_qkv)], [tab(cos_f), tab(sin_f)], _epi_rope_full,
                 2 * aw, BF16, bm_t, 1024, "proj_qk")
    v = _matmul([(a, w_a16, o_qkv + 2 * aw)], [], _epi_plain, aw, BF16, 1024, 1024, "proj_v")

    cos_i, sin_i, half = _rope_tables(seq, IDX_ROPE_DIM, IDX_DIM)
    zeros = jnp.zeros((seq, half), F32)
    cos_i = jnp.concatenate([cos_i, cos_i, jnp.ones((seq, IDX_DIM - 2 * half), F32)], axis=1)
    sin_a = jnp.concatenate([-sin_i, zeros, jnp.zeros((seq, IDX_DIM - 2 * half), F32)], axis=1)
    sin_b = jnp.concatenate([zeros, sin_i, jnp.zeros((seq, IDX_DIM - 2 * half), F32)], axis=1)
    itabs = [tab(cos_i), tab(sin_a), tab(sin_b)]
    qi = _matmul([(a, w_a16, o_qi)], itabs, _epi_rope_idx, n_idx * IDX_DIM, BF16, bm_t, 1024, "proj_qi",
                 head_major=True)
    kiwi = _matmul([(a, w_a16, o_ki)], itabs, _epi_kiwi, 2 * IDX_DIM, F32, bm_t, 2 * IDX_DIM, "proj_kiwi")

    gates = _matmul([(a, w_g16)],
                    [(b_gate.reshape(1, 2 * d), (1, _pick(1024, 2 * d)), lambda i, j: (0, j))],
                    _epi_sigmoid_bias, 2 * d, BF16, 1024, 1024, "proj_gates")

    g_ssm = _s5_branch(u, bsz, seq, lam_re, lam_im, log_dt, b_re, b_im, c_re, c_im, d_skip)
    bn_g = _pick(1024, ssm_w)
    y_ssm = _matmul([(g_ssm, w_glu.astype(BF16))], [(g_ssm, (_pick(1024, n_tok), bn_g), lambda i, j: (i, j))],
                    _epi_glu, ssm_w, BF16, 1024, 1024, "ssm_glu")

    y_att = _dsa_branch(qk, v, qi, kiwi, bsz, seq, n_heads, n_idx)

    bm_m, bn_m = _pick(512, n_tok), _pick(1024, d)
    nj = d // bn_m
    merged = _matmul([(y_ssm, w_ssm_out.astype(BF16)), (y_att, w_attn_out.astype(BF16))],
                     [(gates, (bm_m, bn_m), lambda i, j: (i, j)), (gates, (bm_m, bn_m), lambda i, j: (i, nj + j))],
                     _epi_merge, d, BF16, bm_m, bn_m, "merge")
    h1 = _matmul([(merged, w_o.astype(BF16))], [(h, (bm_m, bn_m), lambda i, j: (i, j))],
                 _epi_residual, d, F32, bm_m, bn_m, "out_proj")

    y4, route = _moe_branch(h1, g_moe, w_router, b_router, w1, b1, w2, b2)
    h2, n2 = _combine(h1, y4, route, g_ple)

    h3 = _matmul([(n2, w_ple_gate.astype(BF16)), (p_i.astype(BF16), w_ple.astype(BF16))],
                 [(h2, (bm_m, bn_m), lambda i, j: (i, j)), (b_ple_gate.reshape(1, d), (1, bn_m), lambda i, j: (0, j))],
                 _epi_ple, d, F32, bm_m, bn_m, "ple")
    return h3


def kernel(x, p, g_mix, w_in, b_gate, ssm_lambda_re, ssm_lambda_im, ssm_log_dt, ssm_b_re, ssm_b_im, ssm_c_re, ssm_c_im, ssm_d, ssm_w_glu, w_ssm_out, w_attn_out, w_o, g_moe, w_router, b_router, w_expert_in, b_expert_in, w_expert_out, b_expert_out, g_ple, w_ple, w_ple_gate, b_ple_gate, g_final):
    bsz, seq, d = x.shape
    h = x.reshape(bsz * seq, d)
    for i in range(p.shape[0]):
        h = _layer(h, p[i].reshape(bsz * seq, -1), g_mix[i], w_in[i], b_gate[i], ssm_lambda_re[i],
                   ssm_lambda_im[i], ssm_log_dt[i], ssm_b_re[i], ssm_b_im[i], ssm_c_re[i], ssm_c_im[i],
                   ssm_d[i], ssm_w_glu[i], w_ssm_out[i], w_attn_out[i], w_o[i], g_moe[i], w_router[i],
                   b_router[i], w_expert_in[i], b_expert_in[i], w_expert_out[i], b_expert_out[i],
                   g_ple[i], w_ple[i], w_ple_gate[i], b_ple_gate[i], bsz, seq)
    return _rmsnorm(h, g_final, F32, "rms_final").reshape(bsz, seq, d)
```
